```python
import math
import jax, jax.numpy as jnp
from jax import lax
import numpy as np

D_MODEL = 1024
BATCH = 8
SEQ = 4096
DEPTH = 1

SSD_HEADS = 32
SSD_HEAD_DIM = 64
D_SSD = SSD_HEADS * SSD_HEAD_DIM
SSD_GROUPS = 8
HEADS_PER_GROUP = SSD_HEADS // SSD_GROUPS
D_STATE = 128
CONV_WIDTH = 4
CHUNK = 128
D_BC = SSD_GROUPS * D_STATE
D_CONV = D_SSD + 2 * D_BC
POOL_WINDOWS = (2, 4, 8, 16)
POOL_GROUPS = len(POOL_WINDOWS)
POOL_CH = 256
D_POOL = POOL_GROUPS * POOL_CH
D_MIX = D_SSD + D_POOL
D_IN = D_SSD + D_CONV + SSD_HEADS + D_POOL
N_EXPERTS = 32
TOP_K = 4
D_FF = 1024
SWIGLU_LIMIT = 7.0
SWIGLU_ALPHA = 1.702
EXPERT_BLOCK = 128
PLE_DIM = 256
EPS = 1e-6

kernel_name = 'hybrid_ssd_pool_moe_ple'


def rms_norm(x, g):
    xf = x.astype(jnp.float32)
    y = xf * lax.rsqrt(jnp.mean(xf * xf, axis=-1, keepdims=True) + EPS)
    return y.astype(x.dtype) * g


def ssd_chunked_scan(xd, a, Bm, Cm):
    b, L = xd.shape[:2]
    nc = L // CHUNK

    def to_chunks(t):
        return jnp.moveaxis(t.reshape((b, nc, CHUNK) + t.shape[2:]), 1, 0)

    causal = jnp.tril(jnp.ones((CHUNK, CHUNK), dtype=bool))[None, :, :, None, None]

    def step(state, inp):
        xd_c, a_c, B_c, C_c = inp
        a_cs = jnp.cumsum(a_c, axis=1)
        seg = a_cs[:, :, None] - a_cs[:, None, :]
        decay = jnp.exp(jnp.where(causal, seg, -jnp.inf))
        cb = jnp.einsum('bign,bjgn->bijg', C_c, B_c)
        y_diag = jnp.einsum('bijg,bijgr,bjgrp->bigrp', cb, decay, xd_c)
        y_off = jnp.einsum('bign,bgrpn,bigr->bigrp', C_c, state, jnp.exp(a_cs))
        a_tot = a_cs[:, -1]
        to_end = jnp.exp(a_tot[:, None] - a_cs)
        new_state = state * jnp.exp(a_tot)[..., None, None] + jnp.einsum(
            'bjgn,bjgr,bjgrp->bgrpn', B_c, to_end, xd_c)
        return new_state, y_diag + y_off

    state0 = jnp.zeros((b, SSD_GROUPS, HEADS_PER_GROUP, SSD_HEAD_DIM, D_STATE), jnp.float32)
    _, ys = lax.scan(step, state0, (to_chunks(xd), to_chunks(a), to_chunks(Bm), to_chunks(Cm)))
    return jnp.moveaxis(ys, 0, 1).reshape(b, L, SSD_GROUPS, HEADS_PER_GROUP, SSD_HEAD_DIM)


def ssd_mixer(z, xbc, dt, conv_w, conv_b, dt_bias, a_log, d_skip, norm_g):
    b, L, _ = z.shape
    f32 = jnp.float32
    xbc = lax.conv_general_dilated(
        xbc, conv_w[:, None, :], window_strides=(1,), padding=[(CONV_WIDTH - 1, 0)],
        dimension_numbers=('NWC', 'WIO', 'NWC'), feature_group_count=D_CONV)
    xbc = jax.nn.silu(xbc + conv_b)
    xs, Bm, Cm = jnp.split(xbc, [D_SSD, D_SSD + D_BC], axis=-1)
    xs = xs.reshape(b, L, SSD_GROUPS, HEADS_PER_GROUP, SSD_HEAD_DIM).astype(f32)
    Bm = Bm.reshape(b, L, SSD_GROUPS, D_STATE).astype(f32)
    Cm = Cm.reshape(b, L, SSD_GROUPS, D_STATE).astype(f32)
    dt = jax.nn.softplus((dt + dt_bias).astype(f32)).reshape(b, L, SSD_GROUPS, HEADS_PER_GROUP)
    A = -jnp.exp(a_log.astype(f32)).reshape(SSD_GROUPS, HEADS_PER_GROUP)
    y = ssd_chunked_scan(xs * dt[..., None], dt * A, Bm, Cm)
    y = y + d_skip.astype(f32).reshape(SSD_GROUPS, HEADS_PER_GROUP)[:, :, None] * xs
    y = y.reshape(b, L, D_SSD).astype(z.dtype) * jax.nn.silu(z)
    y = rms_norm(y.reshape(b, L, SSD_GROUPS, D_SSD // SSD_GROUPS),
                 norm_g.reshape(SSD_GROUPS, D_SSD // SSD_GROUPS))
    return y.reshape(b, L, D_SSD)


def pool_mixer(u, pool_w, pool_scale):
    b, L, _ = u.shape
    ug = u.reshape(b, L, POOL_GROUPS, POOL_CH).astype(jnp.float32)
    cs = jnp.pad(jnp.cumsum(ug, axis=1), ((0, 0), (1, 0), (0, 0), (0, 0)))
    pos = jnp.arange(L)
    means = []
    for gi, w in enumerate(POOL_WINDOWS):
        c = cs[:, :, gi]
        lagged = jnp.pad(c[:, :L + 1 - w], ((0, 0), (w - 1, 0), (0, 0)))
        cnt = jnp.minimum(pos + 1, w).astype(jnp.float32)
        means.append((c[:, 1:] - lagged) / cnt[None, :, None])
    pooled = jnp.stack(means, axis=2)
    y = (pooled - ug).astype(u.dtype)
    y = jnp.einsum('blgc,gcd->blgd', y, pool_w).reshape(b, L, D_POOL)
    return y * pool_scale


def moe(h, router_w, router_b, w_gate_up, b_gate_up, w_down, b_down):
    T = h.shape[0]
    TK = T * TOP_K
    logits = (h @ router_w + router_b).astype(jnp.float32)
    top_vals, top_idx = lax.top_k(logits, TOP_K)
    gates = jax.nn.softmax(top_vals, axis=-1)
    flat_e = top_idx.reshape(-1)
    order = jnp.argsort(flat_e)
    sorted_e = flat_e[order]
    tok = (order // TOP_K).astype(jnp.int32)
    counts = jnp.bincount(flat_e, length=N_EXPERTS)
    padded = ((counts + EXPERT_BLOCK - 1) // EXPERT_BLOCK) * EXPERT_BLOCK
    group_start = jnp.cumsum(counts) - counts
    padded_end = jnp.cumsum(padded)
    padded_start = padded_end - padded
    rank = jnp.arange(TK, dtype=jnp.int32) - group_start[sorted_e]
    dest = padded_start[sorted_e] + rank
    n_blocks = -(-TK // EXPERT_BLOCK) + N_EXPERTS
    n_slots = n_blocks * EXPERT_BLOCK
    slot_token = jnp.full((n_slots,), T, jnp.int32).at[dest].set(tok)
    slot_gate = jnp.zeros((n_slots,), jnp.float32).at[dest].set(gates.reshape(-1)[order])
    block_start = jnp.arange(n_blocks, dtype=jnp.int32) * EXPERT_BLOCK
    block_e = jnp.clip(jnp.searchsorted(padded_end, block_start, side='right'),
                       0, N_EXPERTS - 1).astype(jnp.int32)
    h_pad = jnp.concatenate([h, jnp.zeros((1, h.shape[1]), h.dtype)], axis=0)
    xs = h_pad[slot_token].reshape(n_blocks, EXPERT_BLOCK, h.shape[1])

    def expert_block(args):
        xb, e = args
        gu = xb @ w_gate_up[e] + b_gate_up[e]
        g, up = gu[:, :D_FF], gu[:, D_FF:]
        g = jnp.minimum(g, SWIGLU_LIMIT)
        up = jnp.clip(up, -SWIGLU_LIMIT, SWIGLU_LIMIT)
        act = (up + 1.0) * (g * jax.nn.sigmoid(SWIGLU_ALPHA * g))
        return act @ w_down[e] + b_down[e]

    ys = lax.map(expert_block, (xs, block_e)).reshape(n_slots, h.shape[1])
    ys = ys * slot_gate.astype(ys.dtype)[:, None]
    out = jnp.zeros((T + 1, h.shape[1]), ys.dtype).at[slot_token].add(ys)
    return out[:T]


def setup_inputs(seed: int = 0) -> dict:
    key = jax.random.key(seed)
    ks = jax.random.split(key, 26)
    f32 = jnp.float32

    def nrm(k, shape, scale):
        return jax.random.normal(k, shape, f32) * scale

    def gain(k, shape):
        return 1.0 + 0.05 * jax.random.normal(k, shape, f32)

    dt0 = jnp.exp(jax.random.uniform(ks[6], (DEPTH, SSD_HEADS), f32,
                                     math.log(1e-3), math.log(1e-1)))
    return {
        'x': nrm(ks[0], (BATCH, SEQ, D_MODEL), 1.0),
        'p': nrm(ks[1], (DEPTH, BATCH, SEQ, PLE_DIM), 1.0),
        'mix_norm_g': gain(ks[2], (DEPTH, D_MODEL)),
        'w_in': nrm(ks[3], (DEPTH, D_MODEL, D_IN), D_MODEL ** -0.5),
        'conv_w': nrm(ks[4], (DEPTH, CONV_WIDTH, D_CONV), CONV_WIDTH ** -0.5),
        'conv_b': nrm(ks[5], (DEPTH, D_CONV), 0.01),
        'dt_bias': dt0 + jnp.log(-jnp.expm1(-dt0)),
        'a_log': jnp.log(jax.random.uniform(ks[7], (DEPTH, SSD_HEADS), f32, 1.0, 16.0)),
        'd_skip': gain(ks[8], (DEPTH, SSD_HEADS)),
        'ssd_norm_g': gain(ks[9], (DEPTH, D_SSD)),
        'pool_w': nrm(ks[10], (DEPTH, POOL_GROUPS, POOL_CH, POOL_CH), POOL_CH ** -0.5),
        'pool_scale': gain(ks[11], (DEPTH, D_POOL)),
        'w_out': nrm(ks[12], (DEPTH, D_MIX, D_MODEL), D_MIX ** -0.5),
        'ffn_norm_g': gain(ks[13], (DEPTH, D_MODEL)),
        'router_w': nrm(ks[14], (DEPTH, D_MODEL, N_EXPERTS), D_MODEL ** -0.5),
        'router_b': nrm(ks[15], (DEPTH, N_EXPERTS), 0.01),
        'w_gate_up': nrm(ks[16], (DEPTH, N_EXPERTS, D_MODEL, 2 * D_FF), D_MODEL ** -0.5),
        'b_gate_up': nrm(ks[17], (DEPTH, N_EXPERTS, 2 * D_FF), 0.01),
        'w_down': nrm(ks[18], (DEPTH, N_EXPERTS, D_FF, D_MODEL), D_FF ** -0.5),
        'b_down': nrm(ks[19], (DEPTH, N_EXPERTS, D_MODEL), 0.01),
        'ple_gate_norm_g': gain(ks[20], (DEPTH, D_MODEL)),
        'w_ple_gate': nrm(ks[21], (DEPTH, D_MODEL, D_MODEL), D_MODEL ** -0.5),
        'w_ple_proj': nrm(ks[22], (DEPTH, PLE_DIM, D_MODEL), PLE_DIM ** -0.5),
        'ple_norm_g': gain(ks[23], (DEPTH, D_MODEL)),
        'final_norm_g': gain(ks[24], (D_MODEL,)),
    }


def reference(x, p, mix_norm_g, w_in, conv_w, conv_b, dt_bias, a_log, d_skip, ssd_norm_g,
              pool_w, pool_scale, w_out, ffn_norm_g, router_w, router_b, w_gate_up,
              b_gate_up, w_down, b_down, ple_gate_norm_g, w_ple_gate, w_ple_proj,
              ple_norm_g, final_norm_g):
    b, L, _ = x.shape
    for i in range(DEPTH):
        h = rms_norm(x, mix_norm_g[i])
        u = jnp.einsum('bld,de->ble', h, w_in[i])
        z, xbc, dt, pool_in = jnp.split(
            u, [D_SSD, D_SSD + D_CONV, D_SSD + D_CONV + SSD_HEADS], axis=-1)
        y_ssd = ssd_mixer(z, xbc, dt, conv_w[i], conv_b[i], dt_bias[i], a_log[i],
                          d_skip[i], ssd_norm_g[i])
        y_pool = pool_mixer(pool_in, pool_w[i], pool_scale[i])
        y_mix = jnp.concatenate([y_ssd, y_pool], axis=-1)
        x = x + jnp.einsum('ble,ed->bld', y_mix, w_out[i])
        h = rms_norm(x, ffn_norm_g[i]).reshape(b * L, D_MODEL)
        x = x + moe(h, router_w[i], router_b[i], w_gate_up[i], b_gate_up[i],
                    w_down[i], b_down[i]).reshape(b, L, D_MODEL)
        gate = jax.nn.sigmoid(rms_norm(x, ple_gate_norm_g[i]) @ w_ple_gate[i])
        e = rms_norm(p[i] @ w_ple_proj[i], ple_norm_g[i])
        x = x + e * gate
    return rms_norm(x, final_norm_g)
```

```python
import functools

import jax
import jax.numpy as jnp
from jax import lax
from jax.experimental import pallas as pl
from jax.experimental.pallas import tpu as pltpu

F32 = jnp.float32
BF16 = jnp.bfloat16
I32 = jnp.int32

D_MODEL = 1024
SSD_HEADS = 32
SSD_HEAD_DIM = 64
D_SSD = SSD_HEADS * SSD_HEAD_DIM
SSD_GROUPS = 8
HEADS_PER_GROUP = SSD_HEADS // SSD_GROUPS
GROUP_W = HEADS_PER_GROUP * SSD_HEAD_DIM
D_STATE = 128
CONV_WIDTH = 4
CHUNK = 128
D_BC = SSD_GROUPS * D_STATE
D_CONV = D_SSD + 2 * D_BC
POOL_WINDOWS = (2, 4, 8, 16)
POOL_CH = 256
D_POOL = len(POOL_WINDOWS) * POOL_CH
POOL_HIST = 128
N_EXPERTS = 32
TOP_K = 4
D_FF = 1024
SWIGLU_LIMIT = 7.0
SWIGLU_ALPHA = 1.702
PLE_DIM = 256
EPS = 1e-6

LANES = 128
HEAD_PAD = LANES

TM_IN = 512
TP_POOL = 256
TM_OUT = 512
TR_ROUTE = 256
BM_EXPERT = 256
VMEM_LIMIT = 56 * 1024 * 1024


def _rms(x, g):
    return x * lax.rsqrt(jnp.mean(x * x, axis=-1, keepdims=True) + EPS) * g


def _silu(x):
    return x * (1.0 / (1.0 + jnp.exp(-x)))


def _split3(v):
    hi = v.astype(BF16)
    r1 = v - hi.astype(F32)
    mid = r1.astype(BF16)
    lo = (r1 - mid.astype(F32)).astype(BF16)
    return hi, mid, lo


W_IN_COLS = D_SSD + D_CONV + D_POOL + HEAD_PAD


def _inproj_kernel(x_ref, g_ref, w_ref, z_ref, xbc_ref, pool_ref, dt_ref):
    hb = _rms(x_ref[...], g_ref[...]).astype(BF16)

    def mm(c0, n):
        return jnp.dot(hb, w_ref[:, c0:c0 + n], preferred_element_type=F32)

    cw = 512
    for c in range(0, D_SSD, cw):
        z_ref[:, c:c + cw] = mm(c, cw).astype(BF16)
    for c in range(0, D_CONV, cw):
        xbc_ref[:, c:c + cw] = mm(D_SSD + c, cw).astype(BF16)
    for c in range(0, D_POOL, cw):
        pool_ref[:, c:c + cw] = mm(D_SSD + D_CONV + c, cw).astype(BF16)
    dt_ref[...] = mm(D_SSD + D_CONV + D_POOL, HEAD_PAD)


def _inproj(x2d, g, w_cat):
    t = x2d.shape[0]
    tm = min(TM_IN, t)
    return pl.pallas_call(
        _inproj_kernel,
        grid=(t // tm,),
        in_specs=[
            pl.BlockSpec((tm, D_MODEL), lambda i: (i, 0)),
            pl.BlockSpec((1, D_MODEL), lambda i: (0, 0)),
            pl.BlockSpec((D_MODEL, W_IN_COLS), lambda i: (0, 0), pipeline_mode=pl.Buffered(1)),
        ],
        out_specs=[
            pl.BlockSpec((tm, D_SSD), lambda i: (i, 0)),
            pl.BlockSpec((tm, D_CONV), lambda i: (i, 0)),
            pl.BlockSpec((tm, D_POOL), lambda i: (i, 0)),
            pl.BlockSpec((tm, HEAD_PAD), lambda i: (i, 0)),
        ],
        out_shape=[
            jax.ShapeDtypeStruct((t, D_SSD), BF16),
            jax.ShapeDtypeStruct((t, D_CONV), BF16),
            jax.ShapeDtypeStruct((t, D_POOL), BF16),
            jax.ShapeDtypeStruct((t, HEAD_PAD), F32),
        ],
        compiler_params=pltpu.CompilerParams(
            dimension_semantics=("arbitrary",), vmem_limit_bytes=VMEM_LIMIT),
        name="inproj",
    )(x2d, g, w_cat)


def _ssd_kernel(xbc_ref, z_ref, dt_ref, convw_ref, convb_ref, dtb_ref, alog_ref, dskip_ref,
                ng_ref, tri_ref, y_ref, state_ref, xpad_ref, xs_s, b_s, c_s):
    q = CHUNK

    @pl.when(pl.program_id(1) == 0)
    def _():
        state_ref[...] = jnp.zeros_like(state_ref)
        xpad_ref[0:8, :] = jnp.zeros((8, D_CONV), F32)

    cw = GROUP_W
    for ci in range(D_CONV // cw):
        cs = slice(ci * cw, (ci + 1) * cw)
        cur = xbc_ref[0, :, cs].astype(F32)
        xpad_ref[8:8 + q, cs] = cur
        acc = convb_ref[:, cs] + convw_ref[CONV_WIDTH - 1:CONV_WIDTH, cs] * cur
        for k in range(CONV_WIDTH - 1):
            off = 8 - (CONV_WIDTH - 1) + k
            acc = acc + convw_ref[k:k + 1, cs] * xpad_ref[off:off + q, cs]
        v = _silu(acc)
        if ci < D_SSD // cw:
            xs_s[ci] = v
        elif ci < (D_SSD + D_BC) // cw:
            j = ci - D_SSD // cw
            b_s[2 * j] = v[:, :D_STATE]
            b_s[2 * j + 1] = v[:, D_STATE:]
        else:
            j = ci - (D_SSD + D_BC) // cw
            c_s[2 * j] = v[:, :D_STATE].astype(BF16)
            c_s[2 * j + 1] = v[:, D_STATE:].astype(BF16)
    xpad_ref[0:8, :] = xpad_ref[q:q + 8, :]

    xdt = dt_ref[0] + dtb_ref[...]
    dt = jnp.maximum(xdt, 0.0) + jnp.log1p(jnp.exp(-jnp.abs(xdt)))
    a = dt * (-jnp.exp(alog_ref[...]))
    tri = tri_ref[...]
    a_hi, a_mid, a_lo = _split3(a)
    acs = (jnp.dot(tri, a_hi, preferred_element_type=F32)
           + jnp.dot(tri, a_mid, preferred_element_type=F32)
           + jnp.dot(tri, a_lo, preferred_element_type=F32))
    acs_t = acs.T
    dt_t = dt.T
    eacs = jnp.exp(acs)
    w_t = jnp.exp(acs_t[:, q - 1:q] - acs_t) * dt_t

    row = lax.broadcasted_iota(I32, (q, q), 0)
    col = lax.broadcasted_iota(I32, (q, q), 1)
    causal = col <= row
    head_shift = SSD_HEAD_DIM.bit_length() - 1
    lane_head = jnp.right_shift(lax.broadcasted_iota(I32, (q, GROUP_W), 1), head_shift)
    lane_head1 = jnp.right_shift(lax.broadcasted_iota(I32, (1, GROUP_W), 1), head_shift)

    for g in range(SSD_GROUPS):
        xs_g = xs_s[g]
        b_g = b_s[g]
        c_g = c_s[g]
        cb = lax.dot_general(c_g, b_g.astype(BF16), (((1,), (1,)), ((), ())),
                             preferred_element_type=F32)
        bt_g = b_g.T
        s_old = state_ref[g]
        y_off = jnp.dot(c_g, s_old.astype(BF16), preferred_element_type=F32)
        y_diag = jnp.zeros((q, GROUP_W), F32)
        ds = jnp.zeros((D_STATE, GROUP_W), F32)
        esc = jnp.zeros((q, GROUP_W), F32)
        sdec = jnp.zeros((1, GROUP_W), F32)
        for r in range(HEADS_PER_GROUP):
            h = g * HEADS_PER_GROUP + r
            seg = acs[:, h:h + 1] - acs_t[h:h + 1, :]
            dec = jnp.where(causal, jnp.exp(seg), 0.0)
            m_h = (cb * dec * dt_t[h:h + 1, :]).astype(BF16)
            xs_m = jnp.where(lane_head == r, xs_g, 0.0).astype(BF16)
            y_diag = y_diag + jnp.dot(m_h, xs_m, preferred_element_type=F32)
            bts = (bt_g * w_t[h:h + 1, :]).astype(BF16)
            ds = ds + jnp.dot(bts, xs_m, preferred_element_type=F32)
            esc = jnp.where(lane_head == r, eacs[:, h:h + 1], esc)
            sdec = jnp.where(lane_head1 == r, eacs[q - 1:q, h:h + 1], sdec)
        state_ref[g] = s_old * sdec + ds
        gs = slice(g * GROUP_W, (g + 1) * GROUP_W)
        y = y_diag + y_off * esc + dskip_ref[:, gs] * xs_g
        y = y * _silu(z_ref[0, :, gs].astype(F32))
        y_ref[0, :, gs] = _rms(y, ng_ref[:, gs]).astype(BF16)


def _ssd(xbc, z, dt, conv_w, conv_b, dtb, alog, dskip, ng, tri):
    b, l, _ = xbc.shape
    nc = l // CHUNK
    cmap = lambda i, c: (0, 0)
    return pl.pallas_call(
        _ssd_kernel,
        grid=(b, nc),
        in_specs=[
            pl.BlockSpec((1, CHUNK, D_CONV), lambda i, c: (i, c, 0)),
            pl.BlockSpec((1, CHUNK, D_SSD), lambda i, c: (i, c, 0)),
            pl.BlockSpec((1, CHUNK, HEAD_PAD), lambda i, c: (i, c, 0)),
            pl.BlockSpec((CONV_WIDTH, D_CONV), cmap),
            pl.BlockSpec((1, D_CONV), cmap),
            pl.BlockSpec((1, HEAD_PAD), cmap),
            pl.BlockSpec((1, HEAD_PAD), cmap),
            pl.BlockSpec((1, D_SSD), cmap),
            pl.BlockSpec((1, D_SSD), cmap),
            pl.BlockSpec((CHUNK, CHUNK), cmap),
        ],
        out_specs=pl.BlockSpec((1, CHUNK, D_SSD), lambda i, c: (i, c, 0)),
        out_shape=jax.ShapeDtypeStruct((b, l, D_SSD), BF16),
        scratch_shapes=[
            pltpu.VMEM((SSD_GROUPS, D_STATE, GROUP_W), F32),
            pltpu.VMEM((8 + CHUNK, D_CONV), F32),
            pltpu.VMEM((SSD_GROUPS, CHUNK, GROUP_W), F32),
            pltpu.VMEM((SSD_GROUPS, CHUNK, D_STATE), F32),
            pltpu.VMEM((SSD_GROUPS, CHUNK, D_STATE), BF16),
        ],
        compiler_params=pltpu.CompilerParams(
            dimension_semantics=("arbitrary", "arbitrary"), vmem_limit_bytes=VMEM_LIMIT),
        name="ssd",
    )(xbc, z, dt, conv_w, conv_b, dtb, alog, dskip, ng, tri)


def _pool_kernel(u_ref, band_ref, pw_ref, scale_ref, y_ref, hist_ref):
    tp = u_ref.shape[1]
    li = pl.program_id(1)

    @pl.when(li == 0)
    def _():
        hist_ref[0:POOL_HIST, :] = jnp.zeros((POOL_HIST, D_POOL), BF16)

    hist_ref[POOL_HIST:POOL_HIST + tp, :] = u_ref[0]
    pos = li * tp + lax.broadcasted_iota(I32, (tp, POOL_CH), 0)
    for gi, w in enumerate(POOL_WINDOWS):
        gs = slice(gi * POOL_CH, (gi + 1) * POOL_CH)
        win_sum = jnp.dot(band_ref[gi], hist_ref[:, gs], preferred_element_type=F32)
        cnt = jnp.minimum(pos + 1, w).astype(F32)
        pre = (win_sum / cnt - u_ref[0, :, gs].astype(F32)).astype(BF16)
        y = jnp.dot(pre, pw_ref[gi], preferred_element_type=F32) * scale_ref[:, gs]
        y_ref[0, :, gs] = y.astype(BF16)
    hist_ref[0:POOL_HIST, :] = hist_ref[tp:tp + POOL_HIST, :]


def _pool(u, band, pw, scale):
    b, l, _ = u.shape
    tp = band.shape[1]
    return pl.pallas_call(
        _pool_kernel,
        grid=(b, l // tp),
        in_specs=[
            pl.BlockSpec((1, tp, D_POOL), lambda i, j: (i, j, 0)),
            pl.BlockSpec((len(POOL_WINDOWS), tp, tp + POOL_HIST), lambda i, j: (0, 0, 0)),
            pl.BlockSpec((len(POOL_WINDOWS), POOL_CH, POOL_CH), lambda i, j: (0, 0, 0)),
            pl.BlockSpec((1, D_POOL), lambda i, j: (0, 0)),
        ],
        out_specs=pl.BlockSpec((1, tp, D_POOL), lambda i, j: (i, j, 0)),
        out_shape=jax.ShapeDtypeStruct((b, l, D_POOL), BF16),
        scratch_shapes=[pltpu.VMEM((POOL_HIST + tp, D_POOL), BF16)],
        compiler_params=pltpu.CompilerParams(
            dimension_semantics=("arbitrary", "arbitrary"), vmem_limit_bytes=VMEM_LIMIT),
        name="pool",
    )(u, band, pw, scale)


def _outproj_kernel(ys_ref, yp_ref, x_ref, ws_ref, wp_ref, g_ref, rwt_ref, rb_ref,
                    x1_ref, h_ref, lt_ref):
    acc = jnp.dot(ys_ref[...], ws_ref[...], preferred_element_type=F32)
    acc = acc + jnp.dot(yp_ref[...], wp_ref[...], preferred_element_type=F32)
    x1 = x_ref[...] + acc
    x1_ref[...] = x1
    h = _rms(x1, g_ref[...])
    h_ref[...] = h
    lt = lax.dot_general(rwt_ref[...], h.astype(BF16), (((1,), (1,)), ((), ())),
                         preferred_element_type=F32)
    lt_ref[...] = lt + rb_ref[...]


def _outproj(y_ssd, y_pool, x2d, w_s, w_p, g, rwt, rb):
    t = x2d.shape[0]
    tm = min(TM_OUT, t)
    c2 = lambda i: (0, 0)
    return pl.pallas_call(
        _outproj_kernel,
        grid=(t // tm,),
        in_specs=[
            pl.BlockSpec((tm, D_SSD), lambda i: (i, 0)),
            pl.BlockSpec((tm, D_POOL), lambda i: (i, 0)),
            pl.BlockSpec((tm, D_MODEL), lambda i: (i, 0)),
            pl.BlockSpec((D_SSD, D_MODEL), c2),
            pl.BlockSpec((D_POOL, D_MODEL), c2),
            pl.BlockSpec((1, D_MODEL), c2),
            pl.BlockSpec((N_EXPERTS, D_MODEL), c2),
            pl.BlockSpec((N_EXPERTS, 1), c2),
        ],
        out_specs=[
            pl.BlockSpec((tm, D_MODEL), lambda i: (i, 0)),
            pl.BlockSpec((tm, D_MODEL), lambda i: (i, 0)),
            pl.BlockSpec((N_EXPERTS, tm), lambda i: (0, i)),
        ],
        out_shape=[
            jax.ShapeDtypeStruct((t, D_MODEL), F32),
            jax.ShapeDtypeStruct((t, D_MODEL), F32),
            jax.ShapeDtypeStruct((N_EXPERTS, t), F32),
        ],
        compiler_params=pltpu.CompilerParams(
            dimension_semantics=("arbitrary",), vmem_limit_bytes=VMEM_LIMIT),
        name="outproj",
    )(y_ssd, y_pool, x2d, w_s, w_p, g, rwt, rb)


def _route_kernel(lt_ref, utri_ref, idx_ref, gate_ref, rank_ref, cnt_ref, carry_ref):
    tr = lt_ref.shape[1]

    @pl.when(pl.program_id(0) == 0)
    def _():
        carry_ref[...] = jnp.zeros_like(carry_ref)

    l = lt_ref[...]
    eidx = lax.broadcasted_iota(I32, (N_EXPERTS, tr), 0).astype(F32)
    sels, vals, idxs = [], [], []
    for _ in range(TOP_K):
        m = jnp.max(l, axis=0, keepdims=True)
        ik = jnp.min(jnp.where(l == m, eidx, float(N_EXPERTS)), axis=0, keepdims=True)
        sel = eidx == ik
        l = jnp.where(sel, -jnp.inf, l)
        sels.append(sel)
        vals.append(m)
        idxs.append(ik)
    es = [jnp.exp(v - vals[0]) for v in vals]
    den = es[0] + es[1] + es[2] + es[3]
    multi = jnp.zeros((N_EXPERTS, tr), F32)
    for sel in sels:
        multi = multi + jnp.where(sel, 1.0, 0.0)
    incl = jnp.dot(multi.astype(BF16), utri_ref[...], preferred_element_type=F32)
    excl = incl - multi + carry_ref[:, 0:1]
    for k in range(TOP_K):
        idx_ref[0, k:k + 1, :] = idxs[k].astype(I32)
        gate_ref[0, k:k + 1, :] = es[k] / den
        rk = jnp.sum(jnp.where(sels[k], excl, 0.0), axis=0, keepdims=True)
        rank_ref[0, k:k + 1, :] = rk.astype(I32)
    carry_ref[...] = carry_ref[...] + jnp.sum(multi, axis=1, keepdims=True)
    cnt_ref[...] = carry_ref[...]


def _route(lt, utri):
    t = lt.shape[1]
    tr = utri.shape[0]
    nt = t // tr
    o3 = lambda i: (i, 0, 0)
    return pl.pallas_call(
        _route_kernel,
        grid=(nt,),
        in_specs=[
            pl.BlockSpec((N_EXPERTS, tr), lambda i: (0, i)),
            pl.BlockSpec((tr, tr), lambda i: (0, 0)),
        ],
        out_specs=[
            pl.BlockSpec((1, TOP_K, tr), o3),
            pl.BlockSpec((1, TOP_K, tr), o3),
            pl.BlockSpec((1, TOP_K, tr), o3),
            pl.BlockSpec((N_EXPERTS, LANES), lambda i: (0, 0)),
        ],
        out_shape=[
            jax.ShapeDtypeStruct((nt, TOP_K, tr), I32),
            jax.ShapeDtypeStruct((nt, TOP_K, tr), F32),
            jax.ShapeDtypeStruct((nt, TOP_K, tr), I32),
            jax.ShapeDtypeStruct((N_EXPERTS, LANES), F32),
        ],
        scratch_shapes=[pltpu.VMEM((N_EXPERTS, LANES), F32)],
        compiler_params=pltpu.CompilerParams(
            dimension_semantics=("arbitrary",), vmem_limit_bytes=VMEM_LIMIT),
        name="route",
    )(lt, utri)


def _dest_kernel(pstart_ref, idx_ref, rank_ref, dest_ref):
    idx = idx_ref[0]
    base = jnp.zeros_like(idx)
    for e in range(N_EXPERTS):
        base = jnp.where(idx == e, pstart_ref[e], base)
    dest_ref[0] = base + rank_ref[0]


def _dest(pstart, idx, rank):
    nt, _, tr = idx.shape
    o3 = lambda i, ps: (i, 0, 0)
    return pl.pallas_call(
        _dest_kernel,
        grid_spec=pltpu.PrefetchScalarGridSpec(
            num_scalar_prefetch=1,
            grid=(nt,),
            in_specs=[pl.BlockSpec((1, TOP_K, tr), o3), pl.BlockSpec((1, TOP_K, tr), o3)],
            out_specs=pl.BlockSpec((1, TOP_K, tr), o3),
        ),
        out_shape=jax.ShapeDtypeStruct((nt, TOP_K, tr), I32),
        compiler_params=pltpu.CompilerParams(dimension_semantics=("arbitrary",)),
        name="dest",
    )(pstart, idx, rank)


def _dispatch_kernel(cnt_ref, pad_ref, pstart_ref, nu_ref, dest_hbm, h_ref, xs_hbm, dsm, zrow, sem_d,
                     sem, sem_z):
    i = pl.program_id(0)
    tr = h_ref.shape[0]
    bm = zrow.shape[0]
    nb = xs_hbm.shape[0] // bm

    def zero_copy(slot):
        return pltpu.make_async_copy(zrow.at[pl.ds(0, 1), :], xs_hbm.at[pl.ds(slot, 1), :], sem)

    def zero_block_copy(blk):
        return pltpu.make_async_copy(zrow, xs_hbm.at[pl.ds(blk * bm, bm), :], sem_z)

    @pl.when(i == 0)
    def _():
        zrow[...] = jnp.zeros_like(zrow)

        def zstart(blk, c):
            zero_block_copy(blk).start()
            return c

        def zwait(blk, c):
            zero_block_copy(blk).wait()
            return c

        lax.fori_loop(nu_ref[0], nb, zstart, 0)
        lax.fori_loop(nu_ref[0], nb, zwait, 0)
        for e in range(N_EXPERTS):
            lo, hi, base = cnt_ref[e], pad_ref[e], pstart_ref[e]

            def start(j, c, base=base):
                zero_copy(base + j).start()
                return c

            def wait(j, c, base=base):
                zero_copy(base + j).wait()
                return c

            lax.fori_loop(lo, hi, start, 0)
            lax.fori_loop(lo, hi, wait, 0)

    cp = pltpu.make_async_copy(dest_hbm.at[i], dsm, sem_d)
    cp.start()
    cp.wait()

    def row_copy(t, k):
        d = dsm[k * tr + t]
        return pltpu.make_async_copy(h_ref.at[pl.ds(t, 1), :], xs_hbm.at[pl.ds(d, 1), :], sem)

    def start(t, c):
        for k in range(TOP_K):
            row_copy(t, k).start()
        return c

    def wait(t, c):
        for k in range(TOP_K):
            row_copy(t, k).wait()
        return c

    lax.fori_loop(0, tr, start, 0)
    lax.fori_loop(0, tr, wait, 0)


def _dispatch(counts, padded, pstart, n_used, dest2d, h, n_slots):
    nt, w = dest2d.shape
    tr = w // TOP_K
    return pl.pallas_call(
        _dispatch_kernel,
        grid_spec=pltpu.PrefetchScalarGridSpec(
            num_scalar_prefetch=4,
            grid=(nt,),
            in_specs=[
                pl.BlockSpec(memory_space=pl.ANY),
                pl.BlockSpec((tr, D_MODEL), lambda i, *_: (i, 0)),
            ],
            out_specs=pl.BlockSpec(memory_space=pl.ANY),
            scratch_shapes=[
                pltpu.SMEM((w,), I32),
                pltpu.VMEM((BM_EXPERT, D_MODEL), F32),
                pltpu.SemaphoreType.DMA,
                pltpu.SemaphoreType.DMA,
                pltpu.SemaphoreType.DMA,
            ],
        ),
        out_shape=jax.ShapeDtypeStruct((n_slots, D_MODEL), F32),
        compiler_params=pltpu.CompilerParams(dimension_semantics=("arbitrary",)),
        name="dispatch",
    )(counts, padded, pstart, n_used, dest2d, h)


def _expert_kernel(be_ref, nu_ref, xs_ref, wgu_ref, bgu_ref, wd_ref, bd_ref, ys_ref):
    @pl.when(pl.program_id(0) < nu_ref[0])
    def _():
        xb = xs_ref[...].astype(BF16)
        gu = jnp.dot(xb, wgu_ref[0], preferred_element_type=F32) + bgu_ref[0]
        gt = jnp.minimum(gu[:, :D_FF], SWIGLU_LIMIT)
        up = jnp.clip(gu[:, D_FF:], -SWIGLU_LIMIT, SWIGLU_LIMIT)
        act = (up + 1.0) * (gt * (1.0 / (1.0 + jnp.exp(-SWIGLU_ALPHA * gt))))
        ys_ref[...] = jnp.dot(act.astype(BF16), wd_ref[0], preferred_element_type=F32) + bd_ref[0]

    @pl.when(pl.program_id(0) >= nu_ref[0])
    def _():
        ys_ref[...] = jnp.zeros_like(ys_ref)


def _experts(block_e, n_used, xs, wgu, bgu, wd, bd):
    n_slots = xs.shape[0]
    bm = BM_EXPERT
    nb = n_slots // bm
    row = lambda i, be, nu: (jnp.minimum(i, nu[0] - 1), 0)
    orow = lambda i, be, nu: (i, 0)
    wsel = lambda i, be, nu: (be[i], 0, 0)
    return pl.pallas_call(
        _expert_kernel,
        grid_spec=pltpu.PrefetchScalarGridSpec(
            num_scalar_prefetch=2,
            grid=(nb,),
            in_specs=[
                pl.BlockSpec((bm, D_MODEL), row),
                pl.BlockSpec((1, D_MODEL, 2 * D_FF), wsel),
                pl.BlockSpec((1, 1, 2 * D_FF), wsel),
                pl.BlockSpec((1, D_FF, D_MODEL), wsel),
                pl.BlockSpec((1, 1, D_MODEL), wsel),
            ],
            out_specs=pl.BlockSpec((bm, D_MODEL), orow),
        ),
        out_shape=jax.ShapeDtypeStruct((n_slots, D_MODEL), F32),
        compiler_params=pltpu.CompilerParams(
            dimension_semantics=("arbitrary",), vmem_limit_bytes=VMEM_LIMIT),
        name="experts",
    )(block_e, n_used, xs, wgu, bgu, wd, bd)


def _combine_kernel(dest_hbm, ys_hbm, x1_ref, gate_ref, p_ref, gng_ref, wg_ref, wp_ref, png_ref,
                    fng_ref, o_ref, dsm, ybuf, sem_d, sem):
    i = pl.program_id(0)
    tr = x1_ref.shape[0]
    cp = pltpu.make_async_copy(dest_hbm.at[i], dsm, sem_d)
    cp.start()
    cp.wait()

    def row_copy(t, k):
        d = dsm[k * tr + t]
        return pltpu.make_async_copy(ys_hbm.at[pl.ds(d, 1), :], ybuf.at[k, pl.ds(t, 1), :], sem)

    def start(t, c):
        for k in range(TOP_K):
            row_copy(t, k).start()
        return c

    def wait(t, c):
        for k in range(TOP_K):
            row_copy(t, k).wait()
        return c

    lax.fori_loop(0, tr, start, 0)
    lax.fori_loop(0, tr, wait, 0)

    x2 = x1_ref[...]
    for k in range(TOP_K):
        x2 = x2 + gate_ref[:, k:k + 1] * ybuf[k]
    gate = jnp.dot(_rms(x2, gng_ref[...]).astype(BF16), wg_ref[...], preferred_element_type=F32)
    gate = 1.0 / (1.0 + jnp.exp(-gate))
    e = jnp.dot(p_ref[...].astype(BF16), wp_ref[...], preferred_element_type=F32)
    x3 = x2 + _rms(e, png_ref[...]) * gate
    o_ref[...] = _rms(x3, fng_ref[...])


def _combine(dest2d, ys, x1, gate_t, p2d, gng, wg, wp, png, fng):
    t = x1.shape[0]
    nt, w = dest2d.shape
    tr = w // TOP_K
    c2 = lambda i: (0, 0)
    return pl.pallas_call(
        _combine_kernel,
        grid=(nt,),
        in_specs=[
            pl.BlockSpec(memory_space=pl.ANY),
            pl.BlockSpec(memory_space=pl.ANY),
            pl.BlockSpec((tr, D_MODEL), lambda i: (i, 0)),
            pl.BlockSpec((tr, TOP_K), lambda i: (i, 0)),
            pl.BlockSpec((tr, PLE_DIM), lambda i: (i, 0)),
            pl.BlockSpec((1, D_MODEL), c2),
            pl.BlockSpec((D_MODEL, D_MODEL), c2),
            pl.BlockSpec((PLE_DIM, D_MODEL), c2),
            pl.BlockSpec((1, D_MODEL), c2),
            pl.BlockSpec((1, D_MODEL), c2),
        ],
        out_specs=pl.BlockSpec((tr, D_MODEL), lambda i: (i, 0)),
        out_shape=jax.ShapeDtypeStruct((t, D_MODEL), F32),
        scratch_shapes=[
            pltpu.SMEM((w,), I32),
            pltpu.VMEM((TOP_K, tr, D_MODEL), F32),
            pltpu.SemaphoreType.DMA,
            pltpu.SemaphoreType.DMA,
        ],
        compiler_params=pltpu.CompilerParams(
            dimension_semantics=("arbitrary",), vmem_limit_bytes=VMEM_LIMIT),
        name="combine",
    )(dest2d, ys, x1, gate_t, p2d, gng, wg, wp, png, fng)


def _pad_heads(v):
    return jnp.pad(v.astype(F32), (0, HEAD_PAD - SSD_HEADS)).reshape(1, HEAD_PAD)


def _layer(x, p, mix_norm_g, w_in, conv_w, conv_b, dt_bias, a_log, d_skip, ssd_norm_g, pool_w,
           pool_scale, w_out, ffn_norm_g, router_w, router_b, w_gate_up, b_gate_up, w_down,
           b_down, ple_gate_norm_g, w_ple_gate, w_ple_proj, ple_norm_g, final_g):
    b, l, _ = x.shape
    t = b * l
    x2d = x.reshape(t, D_MODEL)
    row = lambda v: v.reshape(1, -1).astype(F32)

    c_dt = D_SSD + D_CONV
    c_pool = c_dt + SSD_HEADS
    w_cat = jnp.concatenate(
        [w_in[:, :c_dt], w_in[:, c_pool:], w_in[:, c_dt:c_pool],
         jnp.zeros((D_MODEL, HEAD_PAD - SSD_HEADS), w_in.dtype)], axis=1).astype(BF16)
    ii = jnp.arange(CHUNK)
    tri = (ii[None, :] <= ii[:, None]).astype(BF16)
    tp = min(TP_POOL, l)
    rr = jnp.arange(tp)[:, None] + POOL_HIST
    jj = jnp.arange(tp + POOL_HIST)[None, :]
    band = jnp.stack([((jj <= rr) & (jj > rr - w)) for w in POOL_WINDOWS]).astype(BF16)
    tr = min(TR_ROUTE, t)
    ri = jnp.arange(tr)
    utri = (ri[:, None] <= ri[None, :]).astype(BF16)

    z, xbc, pool_in, dt = _inproj(x2d, row(mix_norm_g), w_cat)
    y_ssd = _ssd(xbc.reshape(b, l, D_CONV), z.reshape(b, l, D_SSD), dt.reshape(b, l, HEAD_PAD),
                 conv_w.astype(F32), row(conv_b), _pad_heads(dt_bias), _pad_heads(a_log),
                 row(jnp.repeat(d_skip, SSD_HEAD_DIM)), row(ssd_norm_g), tri)
    y_pool = _pool(pool_in.reshape(b, l, D_POOL), band, pool_w.astype(BF16), row(pool_scale))
    x1, h, lt = _outproj(y_ssd.reshape(t, D_SSD), y_pool.reshape(t, D_POOL), x2d,
                         w_out[:D_SSD].astype(BF16), w_out[D_SSD:].astype(BF16), row(ffn_norm_g),
                         router_w.T.astype(BF16), router_b.reshape(N_EXPERTS, 1).astype(F32))

    idx, gate, rank, cnt = _route(lt, utri)
    counts = cnt[:, 0].astype(I32)
    bm = BM_EXPERT
    padded = ((counts + bm - 1) // bm) * bm
    pend = jnp.cumsum(padded)
    pstart = (pend - padded).astype(I32)
    nb = (t * TOP_K) // bm + N_EXPERTS
    n_used = (pend[-1] // bm).astype(I32)
    blk = jnp.arange(nb, dtype=I32)
    block_e = jnp.clip(jnp.searchsorted(pend, blk * bm, side='right'), 0, N_EXPERTS - 1).astype(I32)
    block_e = jnp.where(blk < n_used, block_e, block_e[n_used - 1])

    dest = _dest(pstart, idx, rank)
    nt = t // tr
    dest2d = dest.reshape(nt, TOP_K * tr)
    xs = _dispatch(counts, padded.astype(I32), pstart, n_used.reshape(1), dest2d, h, nb * bm)
    ys = _experts(block_e, n_used.reshape(1), xs, w_gate_up.astype(BF16),
                  b_gate_up.reshape(N_EXPERTS, 1, 2 * D_FF).astype(F32), w_down.astype(BF16),
                  b_down.reshape(N_EXPERTS, 1, D_MODEL).astype(F32))
    gate_t = gate.transpose(0, 2, 1).reshape(t, TOP_K)
    out = _combine(dest2d, ys, x1, gate_t, p.reshape(t, PLE_DIM), row(ple_gate_norm_g),
                   w_ple_gate.astype(BF16), w_ple_proj.astype(BF16), row(ple_norm_g), row(final_g))
    return out.reshape(b, l, D_MODEL)


def kernel(x, p, mix_norm_g, w_in, conv_w, conv_b, dt_bias, a_log, d_skip, ssd_norm_g, pool_w, pool_scale, w_out, ffn_norm_g, router_w, router_b, w_gate_up, b_gate_up, w_down, b_down, ple_gate_norm_g, w_ple_gate, w_ple_proj, ple_norm_g, final_norm_g):
    assert x.shape[-1] == D_MODEL and mix_norm_g.shape[0] == 1
    i = 0
    return _layer(x, p[i], mix_norm_g[i], w_in[i], conv_w[i], conv_b[i], dt_bias[i], a_log[i],
                  d_skip[i], ssd_norm_g[i], pool_w[i], pool_scale[i], w_out[i], ffn_norm_g[i],
                  router_w[i], router_b[i], w_gate_up[i], b_gate_up[i], w_down[i], b_down[i],
                  ple_gate_norm_g[i], w_ple_gate[i], w_ple_proj[i], ple_norm_g[i], final_norm_g)
```

```python
import functools

import jax
import jax.numpy as jnp
from jax import lax
from jax.experimental import pallas as pl
from jax.experimental.pallas import tpu as pltpu

F32 = jnp.float32
BF16 = jnp.bfloat16
I32 = jnp.int32

D_MODEL = 1024
SSD_HEADS = 32
SSD_HEAD_DIM = 64
D_SSD = SSD_HEADS * SSD_HEAD_DIM
SSD_GROUPS = 8
HEADS_PER_GROUP = SSD_HEADS // SSD_GROUPS
GROUP_W = HEADS_PER_GROUP * SSD_HEAD_DIM
D_STATE = 128
CONV_WIDTH = 4
CHUNK = 128
D_BC = SSD_GROUPS * D_STATE
D_CONV = D_SSD + 2 * D_BC
POOL_WINDOWS = (2, 4, 8, 16)
POOL_CH = 256
D_POOL = len(POOL_WINDOWS) * POOL_CH
POOL_HIST = 128
N_EXPERTS = 32
TOP_K = 4
D_FF = 1024
SWIGLU_LIMIT = 7.0
SWIGLU_ALPHA = 1.702
PLE_DIM = 256
EPS = 1e-6

LANES = 128
HEAD_PAD = LANES

TM_IN = 512
TP_POOL = 256
TM_OUT = 512
TR_ROUTE = 256
BM_EXPERT = 256
VMEM_LIMIT = 56 * 1024 * 1024


def _rms(x, g):
    return x * lax.rsqrt(jnp.mean(x * x, axis=-1, keepdims=True) + EPS) * g


def _silu(x):
    return x * (1.0 / (1.0 + jnp.exp(-x)))


ROW_TILE = D_MODEL // LANES


def _store_row_tiles(ref, v):
    m = v.shape[0]
    for c in range(ROW_TILE):
        ref[pl.ds(c, m, stride=ROW_TILE), :] = v[:, c * LANES:(c + 1) * LANES]


def _load_row_tile_chunk(ref, c, m):
    return ref[pl.ds(c, m, stride=ROW_TILE), :]


def _split3(v):
    hi = v.astype(BF16)
    r1 = v - hi.astype(F32)
    mid = r1.astype(BF16)
    lo = (r1 - mid.astype(F32)).astype(BF16)
    return hi, mid, lo


W_IN_COLS = D_SSD + D_CONV + D_POOL + HEAD_PAD


def _inproj_kernel(x_ref, g_ref, w_ref, z_ref, xbc_ref, pool_ref, dt_ref):
    hb = _rms(x_ref[...], g_ref[...]).astype(BF16)

    def mm(c0, n):
        return jnp.dot(hb, w_ref[:, c0:c0 + n], preferred_element_type=F32)

    cw = 512
    for c in range(0, D_SSD, cw):
        z_ref[:, c:c + cw] = mm(c, cw).astype(BF16)
    for c in range(0, D_CONV, cw):
        xbc_ref[:, c:c + cw] = mm(D_SSD + c, cw).astype(BF16)
    for c in range(0, D_POOL, cw):
        pool_ref[:, c:c + cw] = mm(D_SSD + D_CONV + c, cw).astype(BF16)
    dt_ref[...] = mm(D_SSD + D_CONV + D_POOL, HEAD_PAD)


def _inproj(x2d, g, w_cat):
    t = x2d.shape[0]
    tm = min(TM_IN, t)
    return pl.pallas_call(
        _inproj_kernel,
        grid=(t // tm,),
        in_specs=[
            pl.BlockSpec((tm, D_MODEL), lambda i: (i, 0)),
            pl.BlockSpec((1, D_MODEL), lambda i: (0, 0)),
            pl.BlockSpec((D_MODEL, W_IN_COLS), lambda i: (0, 0), pipeline_mode=pl.Buffered(1)),
        ],
        out_specs=[
            pl.BlockSpec((tm, D_SSD), lambda i: (i, 0)),
            pl.BlockSpec((tm, D_CONV), lambda i: (i, 0)),
            pl.BlockSpec((tm, D_POOL), lambda i: (i, 0)),
            pl.BlockSpec((tm, HEAD_PAD), lambda i: (i, 0)),
        ],
        out_shape=[
            jax.ShapeDtypeStruct((t, D_SSD), BF16),
            jax.ShapeDtypeStruct((t, D_CONV), BF16),
            jax.ShapeDtypeStruct((t, D_POOL), BF16),
            jax.ShapeDtypeStruct((t, HEAD_PAD), F32),
        ],
        compiler_params=pltpu.CompilerParams(
            dimension_semantics=("arbitrary",), vmem_limit_bytes=VMEM_LIMIT),
        name="inproj",
    )(x2d, g, w_cat)


def _ssd_kernel(xbc_ref, z_ref, dt_ref, convw_ref, convb_ref, dtb_ref, alog_ref, dskip_ref,
                ng_ref, tri_ref, y_ref, state_ref, xpad_ref, xs_s, b_s, c_s):
    q = CHUNK

    @pl.when(pl.program_id(1) == 0)
    def _():
        state_ref[...] = jnp.zeros_like(state_ref)
        xpad_ref[0:8, :] = jnp.zeros((8, D_CONV), F32)

    cw = GROUP_W
    for ci in range(D_CONV // cw):
        cs = slice(ci * cw, (ci + 1) * cw)
        cur = xbc_ref[0, :, cs].astype(F32)
        xpad_ref[8:8 + q, cs] = cur
        acc = convb_ref[:, cs] + convw_ref[CONV_WIDTH - 1:CONV_WIDTH, cs] * cur
        for k in range(CONV_WIDTH - 1):
            off = 8 - (CONV_WIDTH - 1) + k
            acc = acc + convw_ref[k:k + 1, cs] * xpad_ref[off:off + q, cs]
        v = _silu(acc)
        if ci < D_SSD // cw:
            xs_s[ci] = v
        elif ci < (D_SSD + D_BC) // cw:
            j = ci - D_SSD // cw
            b_s[2 * j] = v[:, :D_STATE]
            b_s[2 * j + 1] = v[:, D_STATE:]
        else:
            j = ci - (D_SSD + D_BC) // cw
            c_s[2 * j] = v[:, :D_STATE].astype(BF16)
            c_s[2 * j + 1] = v[:, D_STATE:].astype(BF16)
    xpad_ref[0:8, :] = xpad_ref[q:q + 8, :]

    xdt = dt_ref[0] + dtb_ref[...]
    dt = jnp.maximum(xdt, 0.0) + jnp.log1p(jnp.exp(-jnp.abs(xdt)))
    a = dt * (-jnp.exp(alog_ref[...]))
    tri = tri_ref[...]
    a_hi, a_mid, a_lo = _split3(a)
    acs = (jnp.dot(tri, a_hi, preferred_element_type=F32)
           + jnp.dot(tri, a_mid, preferred_element_type=F32)
           + jnp.dot(tri, a_lo, preferred_element_type=F32))
    acs_t = acs.T
    dt_t = dt.T
    eacs = jnp.exp(acs)
    w_t = jnp.exp(acs_t[:, q - 1:q] - acs_t) * dt_t

    row = lax.broadcasted_iota(I32, (q, q), 0)
    col = lax.broadcasted_iota(I32, (q, q), 1)
    causal = col <= row
    head_shift = SSD_HEAD_DIM.bit_length() - 1
    lane_head = jnp.right_shift(lax.broadcasted_iota(I32, (q, GROUP_W), 1), head_shift)
    lane_head1 = jnp.right_shift(lax.broadcasted_iota(I32, (1, GROUP_W), 1), head_shift)

    for g in range(SSD_GROUPS):
        xs_g = xs_s[g]
        b_g = b_s[g]
        c_g = c_s[g]
        cb = lax.dot_general(c_g, b_g.astype(BF16), (((1,), (1,)), ((), ())),
                             preferred_element_type=F32)
        bt_g = b_g.T
        s_old = state_ref[g]
        y_off = jnp.dot(c_g, s_old.astype(BF16), preferred_element_type=F32)
        y_diag = jnp.zeros((q, GROUP_W), F32)
        ds = jnp.zeros((D_STATE, GROUP_W), F32)
        esc = jnp.zeros((q, GROUP_W), F32)
        sdec = jnp.zeros((1, GROUP_W), F32)
        for r in range(HEADS_PER_GROUP):
            h = g * HEADS_PER_GROUP + r
            seg = acs[:, h:h + 1] - acs_t[h:h + 1, :]
            dec = jnp.where(causal, jnp.exp(seg), 0.0)
            m_h = (cb * dec * dt_t[h:h + 1, :]).astype(BF16)
            xs_m = jnp.where(lane_head == r, xs_g, 0.0).astype(BF16)
            y_diag = y_diag + jnp.dot(m_h, xs_m, preferred_element_type=F32)
            bts = (bt_g * w_t[h:h + 1, :]).astype(BF16)
            ds = ds + jnp.dot(bts, xs_m, preferred_element_type=F32)
            esc = jnp.where(lane_head == r, eacs[:, h:h + 1], esc)
            sdec = jnp.where(lane_head1 == r, eacs[q - 1:q, h:h + 1], sdec)
        state_ref[g] = s_old * sdec + ds
        gs = slice(g * GROUP_W, (g + 1) * GROUP_W)
        y = y_diag + y_off * esc + dskip_ref[:, gs] * xs_g
        y = y * _silu(z_ref[0, :, gs].astype(F32))
        y_ref[0, :, gs] = _rms(y, ng_ref[:, gs]).astype(BF16)


def _ssd(xbc, z, dt, conv_w, conv_b, dtb, alog, dskip, ng, tri):
    b, l, _ = xbc.shape
    nc = l // CHUNK
    cmap = lambda i, c: (0, 0)
    return pl.pallas_call(
        _ssd_kernel,
        grid=(b, nc),
        in_specs=[
            pl.BlockSpec((1, CHUNK, D_CONV), lambda i, c: (i, c, 0)),
            pl.BlockSpec((1, CHUNK, D_SSD), lambda i, c: (i, c, 0)),
            pl.BlockSpec((1, CHUNK, HEAD_PAD), lambda i, c: (i, c, 0)),
            pl.BlockSpec((CONV_WIDTH, D_CONV), cmap),
            pl.BlockSpec((1, D_CONV), cmap),
            pl.BlockSpec((1, HEAD_PAD), cmap),
            pl.BlockSpec((1, HEAD_PAD), cmap),
            pl.BlockSpec((1, D_SSD), cmap),
            pl.BlockSpec((1, D_SSD), cmap),
            pl.BlockSpec((CHUNK, CHUNK), cmap),
        ],
        out_specs=pl.BlockSpec((1, CHUNK, D_SSD), lambda i, c: (i, c, 0)),
        out_shape=jax.ShapeDtypeStruct((b, l, D_SSD), BF16),
        scratch_shapes=[
            pltpu.VMEM((SSD_GROUPS, D_STATE, GROUP_W), F32),
            pltpu.VMEM((8 + CHUNK, D_CONV), F32),
            pltpu.VMEM((SSD_GROUPS, CHUNK, GROUP_W), F32),
            pltpu.VMEM((SSD_GROUPS, CHUNK, D_STATE), F32),
            pltpu.VMEM((SSD_GROUPS, CHUNK, D_STATE), BF16),
        ],
        compiler_params=pltpu.CompilerParams(
            dimension_semantics=("arbitrary", "arbitrary"), vmem_limit_bytes=VMEM_LIMIT),
        name="ssd",
    )(xbc, z, dt, conv_w, conv_b, dtb, alog, dskip, ng, tri)


def _pool_kernel(u_ref, band_ref, pw_ref, scale_ref, y_ref, hist_ref):
    tp = u_ref.shape[1]
    li = pl.program_id(1)

    @pl.when(li == 0)
    def _():
        hist_ref[0:POOL_HIST, :] = jnp.zeros((POOL_HIST, D_POOL), BF16)

    hist_ref[POOL_HIST:POOL_HIST + tp, :] = u_ref[0]
    pos = li * tp + lax.broadcasted_iota(I32, (tp, POOL_CH), 0)
    for gi, w in enumerate(POOL_WINDOWS):
        gs = slice(gi * POOL_CH, (gi + 1) * POOL_CH)
        win_sum = jnp.dot(band_ref[gi], hist_ref[:, gs], preferred_element_type=F32)
        cnt = jnp.minimum(pos + 1, w).astype(F32)
        pre = (win_sum / cnt - u_ref[0, :, gs].astype(F32)).astype(BF16)
        y = jnp.dot(pre, pw_ref[gi], preferred_element_type=F32) * scale_ref[:, gs]
        y_ref[0, :, gs] = y.astype(BF16)
    hist_ref[0:POOL_HIST, :] = hist_ref[tp:tp + POOL_HIST, :]


def _pool(u, band, pw, scale):
    b, l, _ = u.shape
    tp = band.shape[1]
    return pl.pallas_call(
        _pool_kernel,
        grid=(b, l // tp),
        in_specs=[
            pl.BlockSpec((1, tp, D_POOL), lambda i, j: (i, j, 0)),
            pl.BlockSpec((len(POOL_WINDOWS), tp, tp + POOL_HIST), lambda i, j: (0, 0, 0)),
            pl.BlockSpec((len(POOL_WINDOWS), POOL_CH, POOL_CH), lambda i, j: (0, 0, 0)),
            pl.BlockSpec((1, D_POOL), lambda i, j: (0, 0)),
        ],
        out_specs=pl.BlockSpec((1, tp, D_POOL), lambda i, j: (i, j, 0)),
        out_shape=jax.ShapeDtypeStruct((b, l, D_POOL), BF16),
        scratch_shapes=[pltpu.VMEM((POOL_HIST + tp, D_POOL), BF16)],
        compiler_params=pltpu.CompilerParams(
            dimension_semantics=("arbitrary", "arbitrary"), vmem_limit_bytes=VMEM_LIMIT),
        name="pool",
    )(u, band, pw, scale)


def _outproj_kernel(ys_ref, yp_ref, x_ref, ws_ref, wp_ref, g_ref, rwt_ref, rb_ref,
                    x1_ref, h_ref, lt_ref):
    acc = jnp.dot(ys_ref[...], ws_ref[...], preferred_element_type=F32)
    acc = acc + jnp.dot(yp_ref[...], wp_ref[...], preferred_element_type=F32)
    x1 = x_ref[...] + acc
    x1_ref[...] = x1
    h = _rms(x1, g_ref[...])
    _store_row_tiles(h_ref, h)
    lt = lax.dot_general(rwt_ref[...], h.astype(BF16), (((1,), (1,)), ((), ())),
                         preferred_element_type=F32)
    lt_ref[...] = lt + rb_ref[...]


def _outproj(y_ssd, y_pool, x2d, w_s, w_p, g, rwt, rb):
    t = x2d.shape[0]
    tm = min(TM_OUT, t)
    c2 = lambda i: (0, 0)
    return pl.pallas_call(
        _outproj_kernel,
        grid=(t // tm,),
        in_specs=[
            pl.BlockSpec((tm, D_SSD), lambda i: (i, 0)),
            pl.BlockSpec((tm, D_POOL), lambda i: (i, 0)),
            pl.BlockSpec((tm, D_MODEL), lambda i: (i, 0)),
            pl.BlockSpec((D_SSD, D_MODEL), c2),
            pl.BlockSpec((D_POOL, D_MODEL), c2),
            pl.BlockSpec((1, D_MODEL), c2),
            pl.BlockSpec((N_EXPERTS, D_MODEL), c2),
            pl.BlockSpec((N_EXPERTS, 1), c2),
        ],
        out_specs=[
            pl.BlockSpec((tm, D_MODEL), lambda i: (i, 0)),
            pl.BlockSpec((tm * ROW_TILE, LANES), lambda i: (i, 0)),
            pl.BlockSpec((N_EXPERTS, tm), lambda i: (0, i)),
        ],
        out_shape=[
            jax.ShapeDtypeStruct((t, D_MODEL), F32),
            jax.ShapeDtypeStruct((t * ROW_TILE, LANES), F32),
            jax.ShapeDtypeStruct((N_EXPERTS, t), F32),
        ],
        compiler_params=pltpu.CompilerParams(
            dimension_semantics=("arbitrary",), vmem_limit_bytes=VMEM_LIMIT),
        name="outproj",
    )(y_ssd, y_pool, x2d, w_s, w_p, g, rwt, rb)


def _route_kernel(lt_ref, utri_ref, idx_ref, gate_ref, rank_ref, cnt_ref, carry_ref):
    tr = lt_ref.shape[1]

    @pl.when(pl.program_id(0) == 0)
    def _():
        carry_ref[...] = jnp.zeros_like(carry_ref)

    l = lt_ref[...]
    eidx = lax.broadcasted_iota(I32, (N_EXPERTS, tr), 0).astype(F32)
    sels, vals, idxs = [], [], []
    for _ in range(TOP_K):
        m = jnp.max(l, axis=0, keepdims=True)
        ik = jnp.min(jnp.where(l == m, eidx, float(N_EXPERTS)), axis=0, keepdims=True)
        sel = eidx == ik
        l = jnp.where(sel, -jnp.inf, l)
        sels.append(sel)
        vals.append(m)
        idxs.append(ik)
    es = [jnp.exp(v - vals[0]) for v in vals]
    den = es[0] + es[1] + es[2] + es[3]
    multi = jnp.zeros((N_EXPERTS, tr), F32)
    for sel in sels:
        multi = multi + jnp.where(sel, 1.0, 0.0)
    incl = jnp.dot(multi.astype(BF16), utri_ref[...], preferred_element_type=F32)
    excl = incl - multi + carry_ref[:, 0:1]
    for k in range(TOP_K):
        idx_ref[0, k:k + 1, :] = idxs[k].astype(I32)
        gate_ref[0, k:k + 1, :] = es[k] / den
        rk = jnp.sum(jnp.where(sels[k], excl, 0.0), axis=0, keepdims=True)
        rank_ref[0, k:k + 1, :] = rk.astype(I32)
    carry_ref[...] = carry_ref[...] + jnp.sum(multi, axis=1, keepdims=True)
    cnt_ref[...] = carry_ref[...]


def _route(lt, utri):
    t = lt.shape[1]
    tr = utri.shape[0]
    nt = t // tr
    o3 = lambda i: (i, 0, 0)
    return pl.pallas_call(
        _route_kernel,
        grid=(nt,),
        in_specs=[
            pl.BlockSpec((N_EXPERTS, tr), lambda i: (0, i)),
            pl.BlockSpec((tr, tr), lambda i: (0, 0)),
        ],
        out_specs=[
            pl.BlockSpec((1, TOP_K, tr), o3),
            pl.BlockSpec((1, TOP_K, tr), o3),
            pl.BlockSpec((1, TOP_K, tr), o3),
            pl.BlockSpec((N_EXPERTS, LANES), lambda i: (0, 0)),
        ],
        out_shape=[
            jax.ShapeDtypeStruct((nt, TOP_K, tr), I32),
            jax.ShapeDtypeStruct((nt, TOP_K, tr), F32),
            jax.ShapeDtypeStruct((nt, TOP_K, tr), I32),
            jax.ShapeDtypeStruct((N_EXPERTS, LANES), F32),
        ],
        scratch_shapes=[pltpu.VMEM((N_EXPERTS, LANES), F32)],
        compiler_params=pltpu.CompilerParams(
            dimension_semantics=("arbitrary",), vmem_limit_bytes=VMEM_LIMIT),
        name="route",
    )(lt, utri)


def _dest_kernel(pstart_ref, idx_ref, rank_ref, dest_ref):
    idx = idx_ref[0]
    base = jnp.zeros_like(idx)
    for e in range(N_EXPERTS):
        base = jnp.where(idx == e, pstart_ref[e], base)
    dest_ref[0] = base + rank_ref[0]


def _dest(pstart, idx, rank):
    nt, _, tr = idx.shape
    o3 = lambda i, ps: (i, 0, 0)
    return pl.pallas_call(
        _dest_kernel,
        grid_spec=pltpu.PrefetchScalarGridSpec(
            num_scalar_prefetch=1,
            grid=(nt,),
            in_specs=[pl.BlockSpec((1, TOP_K, tr), o3), pl.BlockSpec((1, TOP_K, tr), o3)],
            out_specs=pl.BlockSpec((1, TOP_K, tr), o3),
        ),
        out_shape=jax.ShapeDtypeStruct((nt, TOP_K, tr), I32),
        compiler_params=pltpu.CompilerParams(dimension_semantics=("arbitrary",)),
        name="dest",
    )(pstart, idx, rank)


def _row_tile(r):
    return pl.ds(pl.multiple_of(r * ROW_TILE, ROW_TILE), ROW_TILE)


def _dispatch_kernel(cnt_ref, pad_ref, pstart_ref, nu_ref, dest_hbm, h_ref, xs_hbm, dsm, zrow, sem_d,
                     sem, sem_z):
    i = pl.program_id(0)
    w = dsm.shape[0] // 2
    tr = w // TOP_K
    zrows = zrow.shape[0]
    nb = xs_hbm.shape[0] // zrows
    slot = lax.rem(i, 2)

    def dest_copy(step, s):
        return pltpu.make_async_copy(dest_hbm.at[step], dsm.at[pl.ds(pl.multiple_of(s * w, w), w)],
                                     sem_d)

    def zero_copy(r):
        return pltpu.make_async_copy(zrow.at[pl.ds(0, ROW_TILE), :], xs_hbm.at[_row_tile(r), :], sem)

    def zero_block_copy(blk):
        rows = pl.ds(pl.multiple_of(blk * zrows, zrows), zrows)
        return pltpu.make_async_copy(zrow, xs_hbm.at[rows, :], sem_z)

    @pl.when(i == 0)
    def _():
        dest_copy(0, 0).start()
        zrow[...] = jnp.zeros_like(zrow)

        def zstart(blk, c):
            zero_block_copy(blk).start()
            return c

        def zwait(blk, c):
            zero_block_copy(blk).wait()
            return c

        lax.fori_loop(nu_ref[0], nb, zstart, 0)
        lax.fori_loop(nu_ref[0], nb, zwait, 0)
        for e in range(N_EXPERTS):
            lo, hi, base = cnt_ref[e], pad_ref[e], pstart_ref[e]

            def start(j, c, base=base):
                zero_copy(base + j).start()
                return c

            def wait(j, c, base=base):
                zero_copy(base + j).wait()
                return c

            lax.fori_loop(lo, hi, start, 0)
            lax.fori_loop(lo, hi, wait, 0)

    dest_copy(i, slot).wait()

    @pl.when(i + 1 < pl.num_programs(0))
    def _():
        dest_copy(i + 1, 1 - slot).start()

    def row_copy(t, k):
        d = dsm[slot * w + k * tr + t]
        return pltpu.make_async_copy(h_ref.at[_row_tile(t), :], xs_hbm.at[_row_tile(d), :], sem)

    def start(t, c):
        for k in range(TOP_K):
            row_copy(t, k).start(priority=k % 2)
        return c

    def wait(t, c):
        for k in range(TOP_K):
            row_copy(t, k).wait()
        return c

    lax.fori_loop(0, tr, start, 0, unroll=2)
    lax.fori_loop(0, tr, wait, 0, unroll=8)


def _dispatch(counts, padded, pstart, n_used, dest2d, h, n_slots):
    nt, w = dest2d.shape
    tr = w // TOP_K
    return pl.pallas_call(
        _dispatch_kernel,
        grid_spec=pltpu.PrefetchScalarGridSpec(
            num_scalar_prefetch=4,
            grid=(nt,),
            in_specs=[
                pl.BlockSpec(memory_space=pl.ANY),
                pl.BlockSpec((tr * ROW_TILE, LANES), lambda i, *_: (i, 0)),
            ],
            out_specs=pl.BlockSpec(memory_space=pl.ANY),
            scratch_shapes=[
                pltpu.SMEM((2 * w,), I32),
                pltpu.VMEM((BM_EXPERT * ROW_TILE, LANES), F32),
                pltpu.SemaphoreType.DMA,
                pltpu.SemaphoreType.DMA,
                pltpu.SemaphoreType.DMA,
            ],
        ),
        out_shape=jax.ShapeDtypeStruct((n_slots * ROW_TILE, LANES), F32),
        compiler_params=pltpu.CompilerParams(dimension_semantics=("arbitrary",)),
        name="dispatch",
    )(counts, padded, pstart, n_used, dest2d, h)


def _expert_kernel(be_ref, nu_ref, xs_ref, wgu_ref, bgu_ref, wd_ref, bd_ref, ys_ref, xb_s, wgu_s,
                   wd_s):
    i = pl.program_id(0)
    bm = xb_s.shape[0]

    @pl.when((i == 0) | (be_ref[i] != be_ref[jnp.maximum(i - 1, 0)]))
    def _():
        rc = 128
        for r in range(0, D_MODEL, rc):
            wgu_s[r:r + rc, :] = wgu_ref[0, r:r + rc, :].astype(BF16)
        for r in range(0, D_FF, rc):
            wd_s[r:r + rc, :] = wd_ref[0, r:r + rc, :].astype(BF16)

    @pl.when(i < nu_ref[0])
    def _():
        for c in range(ROW_TILE):
            xb_s[:, c * LANES:(c + 1) * LANES] = _load_row_tile_chunk(xs_ref, c, bm).astype(BF16)
        gu = jnp.dot(xb_s[...], wgu_s[...], preferred_element_type=F32) + bgu_ref[0]
        gt = jnp.minimum(gu[:, :D_FF], SWIGLU_LIMIT)
        up = jnp.clip(gu[:, D_FF:], -SWIGLU_LIMIT, SWIGLU_LIMIT)
        act = (up + 1.0) * (gt * (1.0 / (1.0 + jnp.exp(-SWIGLU_ALPHA * gt))))
        y = jnp.dot(act.astype(BF16), wd_s[...], preferred_element_type=F32) + bd_ref[0]
        _store_row_tiles(ys_ref, y)

    @pl.when(i >= nu_ref[0])
    def _():
        ys_ref[...] = jnp.zeros_like(ys_ref)


def _experts(block_e, n_used, xs, wgu, bgu, wd, bd):
    bm = BM_EXPERT
    nb = xs.shape[0] // (bm * ROW_TILE)
    row = lambda i, be, nu: (jnp.minimum(i, nu[0] - 1), 0)
    orow = lambda i, be, nu: (i, 0)
    wsel = lambda i, be, nu: (be[i], 0, 0)
    return pl.pallas_call(
        _expert_kernel,
        grid_spec=pltpu.PrefetchScalarGridSpec(
            num_scalar_prefetch=2,
            grid=(nb,),
            in_specs=[
                pl.BlockSpec((bm * ROW_TILE, LANES), row),
                pl.BlockSpec((1, D_MODEL, 2 * D_FF), wsel),
                pl.BlockSpec((1, 1, 2 * D_FF), wsel),
                pl.BlockSpec((1, D_FF, D_MODEL), wsel),
                pl.BlockSpec((1, 1, D_MODEL), wsel),
            ],
            out_specs=pl.BlockSpec((bm * ROW_TILE, LANES), orow),
            scratch_shapes=[
                pltpu.VMEM((bm, D_MODEL), BF16),
                pltpu.VMEM((D_MODEL, 2 * D_FF), BF16),
                pltpu.VMEM((D_FF, D_MODEL), BF16),
            ],
        ),
        out_shape=jax.ShapeDtypeStruct(xs.shape, F32),
        compiler_params=pltpu.CompilerParams(
            dimension_semantics=("arbitrary",), vmem_limit_bytes=VMEM_LIMIT),
        name="experts",
    )(block_e, n_used, xs, wgu, bgu, wd, bd)


def _combine_kernel(dest_hbm, ys_hbm, x1_ref, gate_ref, p_ref, gng_ref, wg_ref, wp_ref, png_ref,
                    fng_ref, o_ref, dsm, ybuf, sem_d, sem):
    i = pl.program_id(0)
    tr = x1_ref.shape[0]
    w = TOP_K * tr
    slot = lax.rem(i, 2)

    def dest_copy(step, s):
        return pltpu.make_async_copy(dest_hbm.at[step], dsm.at[pl.ds(pl.multiple_of(s * w, w), w)],
                                     sem_d)

    def row_copy(s, t, k):
        d = dsm[s * w + k * tr + t]
        return pltpu.make_async_copy(ys_hbm.at[_row_tile(d), :], ybuf.at[s, k, _row_tile(t), :],
                                     sem.at[s])

    def gather_tile(step, s):
        cp = dest_copy(step, s)
        cp.start()
        cp.wait()

        def start(t, c):
            for k in range(TOP_K):
                row_copy(s, t, k).start(priority=k % 2)
            return c

        lax.fori_loop(0, tr, start, 0, unroll=2)

    @pl.when(i == 0)
    def _():
        gather_tile(0, 0)

    @pl.when(i + 1 < pl.num_programs(0))
    def _():
        gather_tile(i + 1, 1 - slot)

    def wait(t, c):
        for k in range(TOP_K):
            row_copy(slot, t, k).wait()
        return c

    lax.fori_loop(0, tr, wait, 0, unroll=8)

    chunks = []
    for c in range(ROW_TILE):
        acc = x1_ref[:, c * LANES:(c + 1) * LANES]
        for k in range(TOP_K):
            acc = acc + gate_ref[:, k:k + 1] * _load_row_tile_chunk(ybuf.at[slot, k], c, tr)
        chunks.append(acc)
    x2 = jnp.concatenate(chunks, axis=1)
    gate = jnp.dot(_rms(x2, gng_ref[...]).astype(BF16), wg_ref[...], preferred_element_type=F32)
    gate = 1.0 / (1.0 + jnp.exp(-gate))
    e = jnp.dot(p_ref[...].astype(BF16), wp_ref[...], preferred_element_type=F32)
    x3 = x2 + _rms(e, png_ref[...]) * gate
    o_ref[...] = _rms(x3, fng_ref[...])


def _combine(dest2d, ys, x1, gate_t, p2d, gng, wg, wp, png, fng):
    t = x1.shape[0]
    nt, w = dest2d.shape
    tr = w // TOP_K
    c2 = lambda i: (0, 0)
    return pl.pallas_call(
        _combine_kernel,
        grid=(nt,),
        in_specs=[
            pl.BlockSpec(memory_space=pl.ANY),
            pl.BlockSpec(memory_space=pl.ANY),
            pl.BlockSpec((tr, D_MODEL), lambda i: (i, 0)),
            pl.BlockSpec((tr, TOP_K), lambda i: (i, 0)),
            pl.BlockSpec((tr, PLE_DIM), lambda i: (i, 0)),
            pl.BlockSpec((1, D_MODEL), c2),
            pl.BlockSpec((D_MODEL, D_MODEL), c2),
            pl.BlockSpec((PLE_DIM, D_MODEL), c2),
            pl.BlockSpec((1, D_MODEL), c2),
            pl.BlockSpec((1, D_MODEL), c2),
        ],
        out_specs=pl.BlockSpec((tr, D_MODEL), lambda i: (i, 0)),
        out_shape=jax.ShapeDtypeStruct((t, D_MODEL), F32),
        scratch_shapes=[
            pltpu.SMEM((2 * w,), I32),
            pltpu.VMEM((2, TOP_K, tr * ROW_TILE, LANES), F32),
            pltpu.SemaphoreType.DMA,
            pltpu.SemaphoreType.DMA((2,)),
        ],
        compiler_params=pltpu.CompilerParams(
            dimension_semantics=("arbitrary",), vmem_limit_bytes=VMEM_LIMIT),
        name="combine",
    )(dest2d, ys, x1, gate_t, p2d, gng, wg, wp, png, fng)


def _pad_heads(v):
    return jnp.pad(v.astype(F32), (0, HEAD_PAD - SSD_HEADS)).reshape(1, HEAD_PAD)


def _layer(x, p, mix_norm_g, w_in, conv_w, conv_b, dt_bias, a_log, d_skip, ssd_norm_g, pool_w,
           pool_scale, w_out, ffn_norm_g, router_w, router_b, w_gate_up, b_gate_up, w_down,
           b_down, ple_gate_norm_g, w_ple_gate, w_ple_proj, ple_norm_g, final_g):
    b, l, _ = x.shape
    t = b * l
    x2d = x.reshape(t, D_MODEL)
    row = lambda v: v.reshape(1, -1).astype(F32)

    c_dt = D_SSD + D_CONV
    c_pool = c_dt + SSD_HEADS
    w_cat = jnp.concatenate(
        [w_in[:, :c_dt], w_in[:, c_pool:], w_in[:, c_dt:c_pool],
         jnp.zeros((D_MODEL, HEAD_PAD - SSD_HEADS), w_in.dtype)], axis=1).astype(BF16)
    ii = jnp.arange(CHUNK)
    tri = (ii[None, :] <= ii[:, None]).astype(BF16)
    tp = min(TP_POOL, l)
    rr = jnp.arange(tp)[:, None] + POOL_HIST
    jj = jnp.arange(tp + POOL_HIST)[None, :]
    band = jnp.stack([((jj <= rr) & (jj > rr - w)) for w in POOL_WINDOWS]).astype(BF16)
    tr = min(TR_ROUTE, t)
    ri = jnp.arange(tr)
    utri = (ri[:, None] <= ri[None, :]).astype(BF16)

    z, xbc, pool_in, dt = _inproj(x2d, row(mix_norm_g), w_cat)
    y_ssd = _ssd(xbc.reshape(b, l, D_CONV), z.reshape(b, l, D_SSD), dt.reshape(b, l, HEAD_PAD),
                 conv_w.astype(F32), row(conv_b), _pad_heads(dt_bias), _pad_heads(a_log),
                 row(jnp.repeat(d_skip, SSD_HEAD_DIM)), row(ssd_norm_g), tri)
    y_pool = _pool(pool_in.reshape(b, l, D_POOL), band, pool_w.astype(BF16), row(pool_scale))
    x1, h, lt = _outproj(y_ssd.reshape(t, D_SSD), y_pool.reshape(t, D_POOL), x2d,
                         w_out[:D_SSD].astype(BF16), w_out[D_SSD:].astype(BF16), row(ffn_norm_g),
                         router_w.T.astype(BF16), router_b.reshape(N_EXPERTS, 1).astype(F32))

    idx, gate, rank, cnt = _route(lt, utri)
    counts = cnt[:, 0].astype(I32)
    bm = BM_EXPERT
    padded = ((counts + bm - 1) // bm) * bm
    pend = jnp.cumsum(padded)
    pstart = (pend - padded).astype(I32)
    nb = (t * TOP_K) // bm + N_EXPERTS
    n_used = (pend[-1] // bm).astype(I32)
    blk = jnp.arange(nb, dtype=I32)
    block_e = jnp.sum((pend[None, :] <= (blk * bm)[:, None]).astype(I32), axis=1)
    block_e = jnp.minimum(block_e, N_EXPERTS - 1)
    last_e = jnp.sum((pend <= (n_used - 1) * bm).astype(I32))
    block_e = jnp.where(blk < n_used, block_e, jnp.minimum(last_e, N_EXPERTS - 1)).astype(I32)

    dest = _dest(pstart, idx, rank)
    nt = t // tr
    dest2d = dest.reshape(nt, TOP_K * tr)
    xs = _dispatch(counts, padded.astype(I32), pstart, n_used.reshape(1), dest2d, h, nb * bm)
    ys = _experts(block_e, n_used.reshape(1), xs, w_gate_up.astype(F32),
                  b_gate_up.reshape(N_EXPERTS, 1, 2 * D_FF).astype(F32), w_down.astype(F32),
                  b_down.reshape(N_EXPERTS, 1, D_MODEL).astype(F32))
    gate_t = gate.transpose(0, 2, 1).reshape(t, TOP_K)
    out = _combine(dest2d, ys, x1, gate_t, p.reshape(t, PLE_DIM), row(ple_gate_norm_g),
                   w_ple_gate.astype(BF16), w_ple_proj.astype(BF16), row(ple_norm_g), row(final_g))
    return out.reshape(b, l, D_MODEL)


def kernel(x, p, mix_norm_g, w_in, conv_w, conv_b, dt_bias, a_log, d_skip, ssd_norm_g, pool_w, pool_scale, w_out, ffn_norm_g, router_w, router_b, w_gate_up, b_gate_up, w_down, b_down, ple_gate_norm_g, w_ple_gate, w_ple_proj, ple_norm_g, final_norm_g):
    assert x.shape[-1] == D_MODEL and mix_norm_g.shape[0] == 1
    i = 0
    return _layer(x, p[i], mix_norm_g[i], w_in[i], conv_w[i], conv_b[i], dt_bias[i], a_log[i],
                  d_skip[i], ssd_norm_g[i], pool_w[i], pool_scale[i], w_out[i], ffn_norm_g[i],
                  router_w[i], router_b[i], w_gate_up[i], b_gate_up[i], w_down[i], b_down[i],
                  ple_gate_norm_g[i], w_ple_gate[i], w_ple_proj[i], ple_norm_g[i], final_norm_g)
```

```python
import functools

import jax
import jax.numpy as jnp
from jax import lax
from jax.experimental import pallas as pl
from jax.experimental.pallas import tpu as pltpu

F32 = jnp.float32
BF16 = jnp.bfloat16
I32 = jnp.int32

D_MODEL = 1024
SSD_HEADS = 32
SSD_HEAD_DIM = 64
D_SSD = SSD_HEADS * SSD_HEAD_DIM
SSD_GROUPS = 8
HEADS_PER_GROUP = SSD_HEADS // SSD_GROUPS
GROUP_W = HEADS_PER_GROUP * SSD_HEAD_DIM
D_STATE = 128
CONV_WIDTH = 4
CHUNK = 128
D_BC = SSD_GROUPS * D_STATE
D_CONV = D_SSD + 2 * D_BC
POOL_WINDOWS = (2, 4, 8, 16)
POOL_CH = 256
D_POOL = len(POOL_WINDOWS) * POOL_CH
POOL_HIST = 128
N_EXPERTS = 32
TOP_K = 4
D_FF = 1024
SWIGLU_LIMIT = 7.0
SWIGLU_ALPHA = 1.702
PLE_DIM = 256
EPS = 1e-6

LANES = 128
HEAD_PAD = LANES

TM_IN = 512
TP_POOL = 256
TM_OUT = 512
TR_ROUTE = 256
DEST_TILES = 16
BM_EXPERT = 512
VMEM_LIMIT = 56 * 1024 * 1024


def _rms(x, g):
    return x * lax.rsqrt(jnp.mean(x * x, axis=-1, keepdims=True) + EPS) * g


def _silu(x):
    h = 0.5 * x
    return h + h * jnp.tanh(h)


ROW_TILE = D_MODEL // LANES


def _store_row_tiles(ref, v):
    m = v.shape[0]
    for c in range(ROW_TILE):
        ref[pl.ds(c, m, stride=ROW_TILE), :] = v[:, c * LANES:(c + 1) * LANES]


def _load_row_tile_chunk(ref, c, m):
    return ref[pl.ds(c, m, stride=ROW_TILE), :]


def _split3(v):
    hi = v.astype(BF16)
    r1 = v - hi.astype(F32)
    mid = r1.astype(BF16)
    lo = (r1 - mid.astype(F32)).astype(BF16)
    return hi, mid, lo


W_IN_COLS = D_SSD + D_CONV + D_POOL + HEAD_PAD


def _inproj_kernel(x_ref, g_ref, w_ref, z_ref, xbc_ref, pool_ref, dt_ref):
    hb = _rms(x_ref[...], g_ref[...]).astype(BF16)

    def mm(c0, n):
        return jnp.dot(hb, w_ref[:, c0:c0 + n], preferred_element_type=F32)

    cw = 512
    for c in range(0, D_SSD, cw):
        z_ref[:, c:c + cw] = mm(c, cw).astype(BF16)
    for c in range(0, D_CONV, cw):
        xbc_ref[:, c:c + cw] = mm(D_SSD + c, cw).astype(BF16)
    for c in range(0, D_POOL, cw):
        pool_ref[:, c:c + cw] = mm(D_SSD + D_CONV + c, cw).astype(BF16)
    dt_ref[...] = mm(D_SSD + D_CONV + D_POOL, HEAD_PAD)


def _inproj(x2d, g, w_cat):
    t = x2d.shape[0]
    tm = min(TM_IN, t)
    return pl.pallas_call(
        _inproj_kernel,
        grid=(t // tm,),
        in_specs=[
            pl.BlockSpec((tm, D_MODEL), lambda i: (i, 0)),
            pl.BlockSpec((1, D_MODEL), lambda i: (0, 0)),
            pl.BlockSpec((D_MODEL, W_IN_COLS), lambda i: (0, 0), pipeline_mode=pl.Buffered(1)),
        ],
        out_specs=[
            pl.BlockSpec((tm, D_SSD), lambda i: (i, 0)),
            pl.BlockSpec((tm, D_CONV), lambda i: (i, 0)),
            pl.BlockSpec((tm, D_POOL), lambda i: (i, 0)),
            pl.BlockSpec((tm, HEAD_PAD), lambda i: (i, 0)),
        ],
        out_shape=[
            jax.ShapeDtypeStruct((t, D_SSD), BF16),
            jax.ShapeDtypeStruct((t, D_CONV), BF16),
            jax.ShapeDtypeStruct((t, D_POOL), BF16),
            jax.ShapeDtypeStruct((t, HEAD_PAD), F32),
        ],
        compiler_params=pltpu.CompilerParams(
            dimension_semantics=("arbitrary",), vmem_limit_bytes=VMEM_LIMIT),
        name="inproj",
    )(x2d, g, w_cat)


def _ssd_kernel(xbc_ref, z_ref, dt_ref, convw_ref, convb_ref, dtb_ref, alog_ref, dskip_ref,
                ng_ref, tri_ref, expand_ref, y_ref, state_ref, hist_ref, xs_s, b_s, c_s):
    q = CHUNK

    @pl.when(pl.program_id(1) == 0)
    def _():
        state_ref[...] = jnp.zeros_like(state_ref)
        hist_ref[...] = jnp.zeros_like(hist_ref)

    cw = GROUP_W
    hr = hist_ref.shape[0]
    for ci in range(D_CONV // cw):
        cs = slice(ci * cw, (ci + 1) * cw)
        cur = xbc_ref[0, :, cs].astype(F32)
        xe = jnp.concatenate([hist_ref[:, cs], cur], axis=0)
        hist_ref[:, cs] = cur[q - hr:q, :]
        s1 = pltpu.roll(xe, 1, 0)
        u = convw_ref[3:4, cs] * xe + convw_ref[2:3, cs] * s1
        w = convw_ref[1:2, cs] * xe + convw_ref[0:1, cs] * s1
        acc = convb_ref[:, cs] + u + pltpu.roll(w, 2, 0)
        v = _silu(acc[hr:, :])
        if ci < D_SSD // cw:
            xs_s[ci] = v
        elif ci < (D_SSD + D_BC) // cw:
            j = ci - D_SSD // cw
            b_s[2 * j] = v[:, :D_STATE]
            b_s[2 * j + 1] = v[:, D_STATE:]
        else:
            j = ci - (D_SSD + D_BC) // cw
            c_s[2 * j] = v[:, :D_STATE].astype(BF16)
            c_s[2 * j + 1] = v[:, D_STATE:].astype(BF16)

    xdt = dt_ref[0] + dtb_ref[...]
    dt = jnp.maximum(xdt, 0.0) + jnp.log1p(jnp.exp(-jnp.abs(xdt)))
    a = dt * (-jnp.exp(alog_ref[...]))
    tri = tri_ref[...]
    a_hi, a_mid, a_lo = _split3(a)
    acs = (jnp.dot(tri, a_hi, preferred_element_type=F32)
           + jnp.dot(tri, a_mid, preferred_element_type=F32)
           + jnp.dot(tri, a_lo, preferred_element_type=F32))
    acs_t = acs.T
    dt_t = dt.T
    eacs = jnp.exp(acs)
    w_t = jnp.exp(acs_t[:, q - 1:q] - acs_t) * dt_t
    e_hi = eacs.astype(BF16)
    e_lo = (eacs - e_hi.astype(F32)).astype(BF16)

    row = lax.broadcasted_iota(I32, (q, q), 0)
    col = lax.broadcasted_iota(I32, (q, q), 1)
    causal = col <= row
    head_shift = SSD_HEAD_DIM.bit_length() - 1
    lane_head = jnp.right_shift(lax.broadcasted_iota(I32, (q, GROUP_W), 1), head_shift)
    head_mask = [jnp.where(lane_head == r, 1.0, 0.0).astype(BF16) for r in range(HEADS_PER_GROUP)]

    for g in range(SSD_GROUPS):
        gs = slice(g * GROUP_W, (g + 1) * GROUP_W)
        xs_g = xs_s[g]
        b_g = b_s[g]
        c_g = c_s[g]
        cb = lax.dot_general(c_g, b_g.astype(BF16), (((1,), (1,)), ((), ())),
                             preferred_element_type=F32)
        bt_g = b_g.T
        s_old = state_ref[g]
        y_off = jnp.dot(c_g, s_old.astype(BF16), preferred_element_type=F32)
        esc = (jnp.dot(e_hi, expand_ref[:, gs], preferred_element_type=F32)
               + jnp.dot(e_lo, expand_ref[:, gs], preferred_element_type=F32))
        xs_b = xs_g.astype(BF16)
        m_cat, bts_cat = [], []
        for r in range(HEADS_PER_GROUP):
            h = g * HEADS_PER_GROUP + r
            seg = acs[:, h:h + 1] - acs_t[h:h + 1, :]
            dec = jnp.where(causal, jnp.exp(seg), 0.0)
            m_cat.append((cb * dec * dt_t[h:h + 1, :]).astype(BF16))
            bts_cat.append((bt_g * w_t[h:h + 1, :]).astype(BF16))
        xs_bd = jnp.concatenate([xs_b * head_mask[r] for r in range(HEADS_PER_GROUP)], axis=0)
        y_diag = jnp.dot(jnp.concatenate(m_cat, axis=1), xs_bd, preferred_element_type=F32)
        ds = jnp.dot(jnp.concatenate(bts_cat, axis=1), xs_bd, preferred_element_type=F32)
        state_ref[g] = s_old * esc[q - 1:q, :] + ds
        y = y_diag + y_off * esc + dskip_ref[:, gs] * xs_g
        y = y * _silu(z_ref[0, :, gs].astype(F32))
        y_ref[0, :, gs] = _rms(y, ng_ref[:, gs]).astype(BF16)


def _ssd(xbc, z, dt, conv_w, conv_b, dtb, alog, dskip, ng, tri, expand):
    b, l, _ = xbc.shape
    nc = l // CHUNK
    cmap = lambda i, c: (0, 0)
    return pl.pallas_call(
        _ssd_kernel,
        grid=(b, nc),
        in_specs=[
            pl.BlockSpec((1, CHUNK, D_CONV), lambda i, c: (i, c, 0)),
            pl.BlockSpec((1, CHUNK, D_SSD), lambda i, c: (i, c, 0)),
            pl.BlockSpec((1, CHUNK, HEAD_PAD), lambda i, c: (i, c, 0)),
            pl.BlockSpec((CONV_WIDTH, D_CONV), cmap),
            pl.BlockSpec((1, D_CONV), cmap),
            pl.BlockSpec((1, HEAD_PAD), cmap),
            pl.BlockSpec((1, HEAD_PAD), cmap),
            pl.BlockSpec((1, D_SSD), cmap),
            pl.BlockSpec((1, D_SSD), cmap),
            pl.BlockSpec((CHUNK, CHUNK), cmap),
            pl.BlockSpec((HEAD_PAD, D_SSD), cmap),
        ],
        out_specs=pl.BlockSpec((1, CHUNK, D_SSD), lambda i, c: (i, c, 0)),
        out_shape=jax.ShapeDtypeStruct((b, l, D_SSD), BF16),
        scratch_shapes=[
            pltpu.VMEM((SSD_GROUPS, D_STATE, GROUP_W), F32),
            pltpu.VMEM((8, D_CONV), F32),
            pltpu.VMEM((SSD_GROUPS, CHUNK, GROUP_W), F32),
            pltpu.VMEM((SSD_GROUPS, CHUNK, D_STATE), F32),
            pltpu.VMEM((SSD_GROUPS, CHUNK, D_STATE), BF16),
        ],
        compiler_params=pltpu.CompilerParams(
            dimension_semantics=("arbitrary", "arbitrary"), vmem_limit_bytes=VMEM_LIMIT),
        name="ssd",
    )(xbc, z, dt, conv_w, conv_b, dtb, alog, dskip, ng, tri, expand)


def _pool_kernel(u_ref, band_ref, pw_ref, scale_ref, y_ref, hist_ref):
    tp = u_ref.shape[1]
    li = pl.program_id(1)

    @pl.when(li == 0)
    def _():
        hist_ref[0:POOL_HIST, :] = jnp.zeros((POOL_HIST, D_POOL), BF16)

    hist_ref[POOL_HIST:POOL_HIST + tp, :] = u_ref[0]
    pos = li * tp + lax.broadcasted_iota(I32, (tp, POOL_CH), 0)
    for gi, w in enumerate(POOL_WINDOWS):
        gs = slice(gi * POOL_CH, (gi + 1) * POOL_CH)
        win_sum = jnp.dot(band_ref[gi], hist_ref[:, gs], preferred_element_type=F32)
        cnt = jnp.minimum(pos + 1, w).astype(F32)
        pre = (win_sum / cnt - u_ref[0, :, gs].astype(F32)).astype(BF16)
        y = jnp.dot(pre, pw_ref[gi], preferred_element_type=F32) * scale_ref[:, gs]
        y_ref[0, :, gs] = y.astype(BF16)
    hist_ref[0:POOL_HIST, :] = hist_ref[tp:tp + POOL_HIST, :]


def _pool(u, band, pw, scale):
    b, l, _ = u.shape
    tp = band.shape[1]
    return pl.pallas_call(
        _pool_kernel,
        grid=(b, l // tp),
        in_specs=[
            pl.BlockSpec((1, tp, D_POOL), lambda i, j: (i, j, 0)),
            pl.BlockSpec((len(POOL_WINDOWS), tp, tp + POOL_HIST), lambda i, j: (0, 0, 0)),
            pl.BlockSpec((len(POOL_WINDOWS), POOL_CH, POOL_CH), lambda i, j: (0, 0, 0)),
            pl.BlockSpec((1, D_POOL), lambda i, j: (0, 0)),
        ],
        out_specs=pl.BlockSpec((1, tp, D_POOL), lambda i, j: (i, j, 0)),
        out_shape=jax.ShapeDtypeStruct((b, l, D_POOL), BF16),
        scratch_shapes=[pltpu.VMEM((POOL_HIST + tp, D_POOL), BF16)],
        compiler_params=pltpu.CompilerParams(
            dimension_semantics=("arbitrary", "arbitrary"), vmem_limit_bytes=VMEM_LIMIT),
        name="pool",
    )(u, band, pw, scale)


def _outproj_kernel(ys_ref, yp_ref, x_ref, ws_ref, wp_ref, g_ref, rwt_ref, rb_ref,
                    x1_ref, h_ref, lt_ref):
    acc = jnp.dot(ys_ref[...], ws_ref[...], preferred_element_type=F32)
    acc = acc + jnp.dot(yp_ref[...], wp_ref[...], preferred_element_type=F32)
    x1 = x_ref[...] + acc
    x1_ref[...] = x1
    h = _rms(x1, g_ref[...])
    _store_row_tiles(h_ref, h)
    lt = lax.dot_general(rwt_ref[...], h.astype(BF16), (((1,), (1,)), ((), ())),
                         preferred_element_type=F32)
    lt_ref[...] = lt + rb_ref[...]


def _outproj(y_ssd, y_pool, x2d, w_s, w_p, g, rwt, rb):
    t = x2d.shape[0]
    tm = min(TM_OUT, t)
    c2 = lambda i: (0, 0)
    return pl.pallas_call(
        _outproj_kernel,
        grid=(t // tm,),
        in_specs=[
            pl.BlockSpec((tm, D_SSD), lambda i: (i, 0)),
            pl.BlockSpec((tm, D_POOL), lambda i: (i, 0)),
            pl.BlockSpec((tm, D_MODEL), lambda i: (i, 0)),
            pl.BlockSpec((D_SSD, D_MODEL), c2),
            pl.BlockSpec((D_POOL, D_MODEL), c2),
            pl.BlockSpec((1, D_MODEL), c2),
            pl.BlockSpec((N_EXPERTS, D_MODEL), c2),
            pl.BlockSpec((N_EXPERTS, 1), c2),
        ],
        out_specs=[
            pl.BlockSpec((tm, D_MODEL), lambda i: (i, 0)),
            pl.BlockSpec((tm * ROW_TILE, LANES), lambda i: (i, 0)),
            pl.BlockSpec((N_EXPERTS, tm), lambda i: (0, i)),
        ],
        out_shape=[
            jax.ShapeDtypeStruct((t, D_MODEL), F32),
            jax.ShapeDtypeStruct((t * ROW_TILE, LANES), F32),
            jax.ShapeDtypeStruct((N_EXPERTS, t), F32),
        ],
        compiler_params=pltpu.CompilerParams(
            dimension_semantics=("arbitrary",), vmem_limit_bytes=VMEM_LIMIT),
        name="outproj",
    )(y_ssd, y_pool, x2d, w_s, w_p, g, rwt, rb)


def _route_kernel(lt_ref, utri_ref, idx_ref, gate_ref, rank_ref, cnt_ref, carry_ref):
    tr = lt_ref.shape[1]

    @pl.when(pl.program_id(0) == 0)
    def _():
        carry_ref[...] = jnp.zeros_like(carry_ref)

    l = lt_ref[...]
    eidx = lax.broadcasted_iota(I32, (N_EXPERTS, tr), 0).astype(F32)
    sels, vals, idxs = [], [], []
    for _ in range(TOP_K):
        m = jnp.max(l, axis=0, keepdims=True)
        ik = jnp.min(jnp.where(l == m, eidx, float(N_EXPERTS)), axis=0, keepdims=True)
        sel = eidx == ik
        l = jnp.where(sel, -jnp.inf, l)
        sels.append(sel)
        vals.append(m)
        idxs.append(ik)
    es = [jnp.exp(v - vals[0]) for v in vals]
    den = es[0] + es[1] + es[2] + es[3]
    multi = jnp.zeros((N_EXPERTS, tr), F32)
    for sel in sels:
        multi = multi + jnp.where(sel, 1.0, 0.0)
    incl = jnp.dot(multi.astype(BF16), utri_ref[...], preferred_element_type=F32)
    excl = incl - multi + carry_ref[:, 0:1]
    for k in range(TOP_K):
        idx_ref[0, k:k + 1, :] = idxs[k].astype(I32)
        gate_ref[0, k:k + 1, :] = es[k] / den
        rk = jnp.sum(jnp.where(sels[k], excl, 0.0), axis=0, keepdims=True)
        rank_ref[0, k:k + 1, :] = rk.astype(I32)
    carry_ref[...] = carry_ref[...] + jnp.sum(multi, axis=1, keepdims=True)
    cnt_ref[...] = carry_ref[...]


def _route(lt, utri):
    t = lt.shape[1]
    tr = utri.shape[0]
    nt = t // tr
    o3 = lambda i: (i, 0, 0)
    return pl.pallas_call(
        _route_kernel,
        grid=(nt,),
        in_specs=[
            pl.BlockSpec((N_EXPERTS, tr), lambda i: (0, i)),
            pl.BlockSpec((tr, tr), lambda i: (0, 0)),
        ],
        out_specs=[
            pl.BlockSpec((1, TOP_K, tr), o3),
            pl.BlockSpec((1, TOP_K, tr), o3),
            pl.BlockSpec((1, TOP_K, tr), o3),
            pl.BlockSpec((N_EXPERTS, LANES), lambda i: (0, 0)),
        ],
        out_shape=[
            jax.ShapeDtypeStruct((nt, TOP_K, tr), I32),
            jax.ShapeDtypeStruct((nt, TOP_K, tr), F32),
            jax.ShapeDtypeStruct((nt, TOP_K, tr), I32),
            jax.ShapeDtypeStruct((N_EXPERTS, LANES), F32),
        ],
        scratch_shapes=[pltpu.VMEM((N_EXPERTS, LANES), F32)],
        compiler_params=pltpu.CompilerParams(
            dimension_semantics=("arbitrary",), vmem_limit_bytes=VMEM_LIMIT),
        name="route",
    )(lt, utri)


def _dest_kernel(pstart_ref, idx_ref, rank_ref, dest_ref):
    idx = idx_ref[...]
    base = jnp.zeros_like(idx)
    for e in range(N_EXPERTS):
        base = jnp.where(idx == e, pstart_ref[e], base)
    dest_ref[...] = base + rank_ref[...]


def _dest(pstart, idx, rank):
    nt, _, tr = idx.shape
    tb = min(DEST_TILES, nt)
    o3 = lambda i, ps: (i, 0, 0)
    return pl.pallas_call(
        _dest_kernel,
        grid_spec=pltpu.PrefetchScalarGridSpec(
            num_scalar_prefetch=1,
            grid=(nt // tb,),
            in_specs=[pl.BlockSpec((tb, TOP_K, tr), o3), pl.BlockSpec((tb, TOP_K, tr), o3)],
            out_specs=pl.BlockSpec((tb, TOP_K, tr), o3),
        ),
        out_shape=jax.ShapeDtypeStruct((nt, TOP_K, tr), I32),
        compiler_params=pltpu.CompilerParams(dimension_semantics=("arbitrary",)),
        name="dest",
    )(pstart, idx, rank)


def _row_tile(r):
    return pl.ds(pl.multiple_of(r * ROW_TILE, ROW_TILE), ROW_TILE)


def _dispatch_kernel(cnt_ref, pad_ref, pstart_ref, nu_ref, dest_hbm, h_ref, xs_hbm, dsm, zrow, sem_d,
                     sem, sem_z):
    i = pl.program_id(0)
    w = dsm.shape[0] // 2
    tr = w // TOP_K
    zrows = zrow.shape[0]
    nb = xs_hbm.shape[0] // zrows
    slot = lax.rem(i, 2)

    def dest_copy(step, s):
        return pltpu.make_async_copy(dest_hbm.at[step], dsm.at[pl.ds(pl.multiple_of(s * w, w), w)],
                                     sem_d)

    def zero_copy(r):
        return pltpu.make_async_copy(zrow.at[pl.ds(0, ROW_TILE), :], xs_hbm.at[_row_tile(r), :], sem)

    def zero_block_copy(blk):
        rows = pl.ds(pl.multiple_of(blk * zrows, zrows), zrows)
        return pltpu.make_async_copy(zrow, xs_hbm.at[rows, :], sem_z)

    @pl.when(i == 0)
    def _():
        dest_copy(0, 0).start()
        zrow[...] = jnp.zeros_like(zrow)

        def zstart(blk, c):
            zero_block_copy(blk).start()
            return c

        def zwait(blk, c):
            zero_block_copy(blk).wait()
            return c

        lax.fori_loop(nu_ref[0], nb, zstart, 0)
        lax.fori_loop(nu_ref[0], nb, zwait, 0)
        for e in range(N_EXPERTS):
            lo, hi, base = cnt_ref[e], pad_ref[e], pstart_ref[e]

            def start(j, c, base=base):
                zero_copy(base + j).start()
                return c

            def wait(j, c, base=base):
                zero_copy(base + j).wait()
                return c

            lax.fori_loop(lo, hi, start, 0)
            lax.fori_loop(lo, hi, wait, 0)

    dest_copy(i, slot).wait()

    @pl.when(i + 1 < pl.num_programs(0))
    def _():
        dest_copy(i + 1, 1 - slot).start()

    def row_copy(t, k):
        d = dsm[slot * w + k * tr + t]
        return pltpu.make_async_copy(h_ref.at[_row_tile(t), :], xs_hbm.at[_row_tile(d), :], sem)

    def start(t, c):
        for k in range(TOP_K):
            row_copy(t, k).start(priority=k % 2)
        return c

    def wait(t, c):
        for k in range(TOP_K):
            row_copy(t, k).wait()
        return c

    lax.fori_loop(0, tr, start, 0, unroll=2)
    lax.fori_loop(0, tr, wait, 0, unroll=8)


def _dispatch(counts, padded, pstart, n_used, dest2d, h, n_slots):
    nt, w = dest2d.shape
    tr = w // TOP_K
    return pl.pallas_call(
        _dispatch_kernel,
        grid_spec=pltpu.PrefetchScalarGridSpec(
            num_scalar_prefetch=4,
            grid=(nt,),
            in_specs=[
                pl.BlockSpec(memory_space=pl.ANY),
                pl.BlockSpec((tr * ROW_TILE, LANES), lambda i, *_: (i, 0)),
            ],
            out_specs=pl.BlockSpec(memory_space=pl.ANY),
            scratch_shapes=[
                pltpu.SMEM((2 * w,), I32),
                pltpu.VMEM((BM_EXPERT * ROW_TILE, LANES), F32),
                pltpu.SemaphoreType.DMA,
                pltpu.SemaphoreType.DMA,
                pltpu.SemaphoreType.DMA,
            ],
        ),
        out_shape=jax.ShapeDtypeStruct((n_slots * ROW_TILE, LANES), F32),
        compiler_params=pltpu.CompilerParams(dimension_semantics=("arbitrary",)),
        name="dispatch",
    )(counts, padded, pstart, n_used, dest2d, h)


def _expert_kernel(be_ref, nu_ref, xs_ref, wgu_ref, bgu_ref, wd_ref, bd_ref, ys_ref, xb_s, wgu_s,
                   wd_s):
    i = pl.program_id(0)
    bm = xb_s.shape[0]

    @pl.when((i == 0) | (be_ref[i] != be_ref[jnp.maximum(i - 1, 0)]))
    def _():
        rc = 128
        for r in range(0, D_MODEL, rc):
            wgu_s[r:r + rc, :] = wgu_ref[0, r:r + rc, :].astype(BF16)
        for r in range(0, D_FF, rc):
            wd_s[r:r + rc, :] = wd_ref[0, r:r + rc, :].astype(BF16)

    @pl.when(i < nu_ref[0])
    def _():
        for c in range(ROW_TILE):
            xb_s[:, c * LANES:(c + 1) * LANES] = _load_row_tile_chunk(xs_ref, c, bm).astype(BF16)
        gu = jnp.dot(xb_s[...], wgu_s[...], preferred_element_type=F32) + bgu_ref[0]
        gt = jnp.minimum(gu[:, :D_FF], SWIGLU_LIMIT)
        up = jnp.clip(gu[:, D_FF:], -SWIGLU_LIMIT, SWIGLU_LIMIT)
        hg = 0.5 * gt
        act = (up + 1.0) * (hg + hg * jnp.tanh(SWIGLU_ALPHA * hg))
        y = jnp.dot(act.astype(BF16), wd_s[...], preferred_element_type=F32) + bd_ref[0]
        _store_row_tiles(ys_ref, y)

    @pl.when(i >= nu_ref[0])
    def _():
        ys_ref[...] = jnp.zeros_like(ys_ref)


def _experts(block_e, n_used, xs, wgu, bgu, wd, bd):
    bm = BM_EXPERT
    nb = xs.shape[0] // (bm * ROW_TILE)
    row = lambda i, be, nu: (jnp.minimum(i, nu[0] - 1), 0)
    orow = lambda i, be, nu: (i, 0)
    wsel = lambda i, be, nu: (be[i], 0, 0)
    return pl.pallas_call(
        _expert_kernel,
        grid_spec=pltpu.PrefetchScalarGridSpec(
            num_scalar_prefetch=2,
            grid=(nb,),
            in_specs=[
                pl.BlockSpec((bm * ROW_TILE, LANES), row),
                pl.BlockSpec((1, D_MODEL, 2 * D_FF), wsel),
                pl.BlockSpec((1, 1, 2 * D_FF), wsel),
                pl.BlockSpec((1, D_FF, D_MODEL), wsel),
                pl.BlockSpec((1, 1, D_MODEL), wsel),
            ],
            out_specs=pl.BlockSpec((bm * ROW_TILE, LANES), orow),
            scratch_shapes=[
                pltpu.VMEM((bm, D_MODEL), BF16),
                pltpu.VMEM((D_MODEL, 2 * D_FF), BF16),
                pltpu.VMEM((D_FF, D_MODEL), BF16),
            ],
        ),
        out_shape=jax.ShapeDtypeStruct(xs.shape, F32),
        compiler_params=pltpu.CompilerParams(
            dimension_semantics=("arbitrary",), vmem_limit_bytes=VMEM_LIMIT),
        name="experts",
    )(block_e, n_used, xs, wgu, bgu, wd, bd)


def _combine_kernel(dest_hbm, ys_hbm, x1_ref, gate_ref, p_ref, gng_ref, wg_ref, wp_ref, png_ref,
                    fng_ref, o_ref, dsm, ybuf, sem_d, sem):
    i = pl.program_id(0)
    tr = x1_ref.shape[0]
    w = TOP_K * tr
    slot = lax.rem(i, 2)

    def dest_copy(step, s):
        return pltpu.make_async_copy(dest_hbm.at[step], dsm.at[pl.ds(pl.multiple_of(s * w, w), w)],
                                     sem_d)

    def row_copy(s, t, k):
        d = dsm[s * w + k * tr + t]
        return pltpu.make_async_copy(ys_hbm.at[_row_tile(d), :], ybuf.at[s, k, _row_tile(t), :],
                                     sem.at[s])

    def gather_tile(step, s):
        cp = dest_copy(step, s)
        cp.start()
        cp.wait()

        def start(t, c):
            for k in range(TOP_K):
                row_copy(s, t, k).start(priority=k % 2)
            return c

        lax.fori_loop(0, tr, start, 0, unroll=2)

    @pl.when(i == 0)
    def _():
        gather_tile(0, 0)

    @pl.when(i + 1 < pl.num_programs(0))
    def _():
        gather_tile(i + 1, 1 - slot)

    def wait(t, c):
        for k in range(TOP_K):
            row_copy(slot, t, k).wait()
        return c

    lax.fori_loop(0, tr, wait, 0, unroll=8)

    chunks = []
    for c in range(ROW_TILE):
        acc = x1_ref[:, c * LANES:(c + 1) * LANES]
        for k in range(TOP_K):
            acc = acc + gate_ref[:, k:k + 1] * _load_row_tile_chunk(ybuf.at[slot, k], c, tr)
        chunks.append(acc)
    x2 = jnp.concatenate(chunks, axis=1)
    gate = jnp.dot(_rms(x2, gng_ref[...]).astype(BF16), wg_ref[...], preferred_element_type=F32)
    gate = 1.0 / (1.0 + jnp.exp(-gate))
    e = jnp.dot(p_ref[...].astype(BF16), wp_ref[...], preferred_element_type=F32)
    x3 = x2 + _rms(e, png_ref[...]) * gate
    o_ref[...] = _rms(x3, fng_ref[...])


def _combine(dest2d, ys, x1, gate_t, p2d, gng, wg, wp, png, fng):
    t = x1.shape[0]
    nt, w = dest2d.shape
    tr = w // TOP_K
    c2 = lambda i: (0, 0)
    return pl.pallas_call(
        _combine_kernel,
        grid=(nt,),
        in_specs=[
            pl.BlockSpec(memory_space=pl.ANY),
            pl.BlockSpec(memory_space=pl.ANY),
            pl.BlockSpec((tr, D_MODEL), lambda i: (i, 0)),
            pl.BlockSpec((tr, TOP_K), lambda i: (i, 0)),
            pl.BlockSpec((tr, PLE_DIM), lambda i: (i, 0)),
            pl.BlockSpec((1, D_MODEL), c2),
            pl.BlockSpec((D_MODEL, D_MODEL), c2),
            pl.BlockSpec((PLE_DIM, D_MODEL), c2),
            pl.BlockSpec((1, D_MODEL), c2),
            pl.BlockSpec((1, D_MODEL), c2),
        ],
        out_specs=pl.BlockSpec((tr, D_MODEL), lambda i: (i, 0)),
        out_shape=jax.ShapeDtypeStruct((t, D_MODEL), F32),
        scratch_shapes=[
            pltpu.SMEM((2 * w,), I32),
            pltpu.VMEM((2, TOP_K, tr * ROW_TILE, LANES), F32),
            pltpu.SemaphoreType.DMA,
            pltpu.SemaphoreType.DMA((2,)),
        ],
        compiler_params=pltpu.CompilerParams(
            dimension_semantics=("arbitrary",), vmem_limit_bytes=VMEM_LIMIT),
        name="combine",
    )(dest2d, ys, x1, gate_t, p2d, gng, wg, wp, png, fng)


def _pad_heads(v):
    return jnp.pad(v.astype(F32), (0, HEAD_PAD - SSD_HEADS)).reshape(1, HEAD_PAD)


def _layer(x, p, mix_norm_g, w_in, conv_w, conv_b, dt_bias, a_log, d_skip, ssd_norm_g, pool_w,
           pool_scale, w_out, ffn_norm_g, router_w, router_b, w_gate_up, b_gate_up, w_down,
           b_down, ple_gate_norm_g, w_ple_gate, w_ple_proj, ple_norm_g, final_g):
    b, l, _ = x.shape
    t = b * l
    x2d = x.reshape(t, D_MODEL)
    row = lambda v: v.reshape(1, -1).astype(F32)

    c_dt = D_SSD + D_CONV
    c_pool = c_dt + SSD_HEADS
    w_cat = jnp.concatenate(
        [w_in[:, :c_dt], w_in[:, c_pool:], w_in[:, c_dt:c_pool],
         jnp.zeros((D_MODEL, HEAD_PAD - SSD_HEADS), w_in.dtype)], axis=1).astype(BF16)
    ii = jnp.arange(CHUNK)
    tri = (ii[None, :] <= ii[:, None]).astype(BF16)
    tp = min(TP_POOL, l)
    rr = jnp.arange(tp)[:, None] + POOL_HIST
    jj = jnp.arange(tp + POOL_HIST)[None, :]
    band = jnp.stack([((jj <= rr) & (jj > rr - w)) for w in POOL_WINDOWS]).astype(BF16)
    tr = min(TR_ROUTE, t)
    ri = jnp.arange(tr)
    utri = (ri[:, None] <= ri[None, :]).astype(BF16)

    head_of_col = jnp.arange(D_SSD) // SSD_HEAD_DIM
    expand = (jnp.arange(HEAD_PAD)[:, None] == head_of_col[None, :]).astype(BF16)

    z, xbc, pool_in, dt = _inproj(x2d, row(mix_norm_g), w_cat)
    y_ssd = _ssd(xbc.reshape(b, l, D_CONV), z.reshape(b, l, D_SSD), dt.reshape(b, l, HEAD_PAD),
                 conv_w.astype(F32), row(conv_b), _pad_heads(dt_bias), _pad_heads(a_log),
                 row(jnp.repeat(d_skip, SSD_HEAD_DIM)), row(ssd_norm_g), tri, expand)
    y_pool = _pool(pool_in.reshape(b, l, D_POOL), band, pool_w.astype(BF16), row(pool_scale))
    x1, h, lt = _outproj(y_ssd.reshape(t, D_SSD), y_pool.reshape(t, D_POOL), x2d,
                         w_out[:D_SSD].astype(BF16), w_out[D_SSD:].astype(BF16), row(ffn_norm_g),
                         router_w.T.astype(BF16), router_b.reshape(N_EXPERTS, 1).astype(F32))

    idx, gate, rank, cnt = _route(lt, utri)
    counts = cnt[:, 0].astype(I32)
    bm = BM_EXPERT
    padded = ((counts + bm - 1) // bm) * bm
    pend = jnp.cumsum(padded)
    pstart = (pend - padded).astype(I32)
    nb = (t * TOP_K) // bm + N_EXPERTS
    n_used = (pend[-1] // bm).astype(I32)
    blk = jnp.arange(nb, dtype=I32)
    block_e = jnp.sum((pend[None, :] <= (blk * bm)[:, None]).astype(I32), axis=1)
    block_e = jnp.minimum(block_e, N_EXPERTS - 1)
    last_e = jnp.sum((pend <= (n_used - 1) * bm).astype(I32))
    block_e = jnp.where(blk < n_used, block_e, jnp.minimum(last_e, N_EXPERTS - 1)).astype(I32)

    dest = _dest(pstart, idx, rank)
    nt = t // tr
    dest2d = dest.reshape(nt, TOP_K * tr)
    xs = _dispatch(counts, padded.astype(I32), pstart, n_used.reshape(1), dest2d, h, nb * bm)
    ys = _experts(block_e, n_used.reshape(1), xs, w_gate_up.astype(F32),
                  b_gate_up.reshape(N_EXPERTS, 1, 2 * D_FF).astype(F32), w_down.astype(F32),
                  b_down.reshape(N_EXPERTS, 1, D_MODEL).astype(F32))
    gate_t = gate.transpose(0, 2, 1).reshape(t, TOP_K)
    out = _combine(dest2d, ys, x1, gate_t, p.reshape(t, PLE_DIM), row(ple_gate_norm_g),
                   w_ple_gate.astype(BF16), w_ple_proj.astype(BF16), row(ple_norm_g), row(final_g))
    return out.reshape(b, l, D_MODEL)


def kernel(x, p, mix_norm_g, w_in, conv_w, conv_b, dt_bias, a_log, d_skip, ssd_norm_g, pool_w, pool_scale, w_out, ffn_norm_g, router_w, router_b, w_gate_up, b_gate_up, w_down, b_down, ple_gate_norm_g, w_ple_gate, w_ple_proj, ple_norm_g, final_norm_g):
    assert x.shape[-1] == D_MODEL and mix_norm_g.shape[0] == 1
    i = 0
    return _layer(x, p[i], mix_norm_g[i], w_in[i], conv_w[i], conv_b[i], dt_bias[i], a_log[i],
                  d_skip[i], ssd_norm_g[i], pool_w[i], pool_scale[i], w_out[i], ffn_norm_g[i],
                  router_w[i], router_b[i], w_gate_up[i], b_gate_up[i], w_down[i], b_down[i],
                  ple_gate_norm_g[i], w_ple_gate[i], w_ple_proj[i], ple_norm_g[i], final_norm_g)
```

```python
import functools

import jax
import jax.numpy as jnp
from jax import lax
from jax.experimental import pallas as pl
from jax.experimental.pallas import tpu as pltpu

F32 = jnp.float32
BF16 = jnp.bfloat16
I32 = jnp.int32

D_MODEL = 1024
SSD_HEADS = 32
SSD_HEAD_DIM = 64
D_SSD = SSD_HEADS * SSD_HEAD_DIM
SSD_GROUPS = 8
HEADS_PER_GROUP = SSD_HEADS // SSD_GROUPS
GROUP_W = HEADS_PER_GROUP * SSD_HEAD_DIM
D_STATE = 128
CONV_WIDTH = 4
CHUNK = 128
D_BC = SSD_GROUPS * D_STATE
D_CONV = D_SSD + 2 * D_BC
POOL_WINDOWS = (2, 4, 8, 16)
POOL_CH = 256
D_POOL = len(POOL_WINDOWS) * POOL_CH
POOL_HIST = 128
N_EXPERTS = 32
TOP_K = 4
D_FF = 1024
SWIGLU_LIMIT = 7.0
SWIGLU_ALPHA = 1.702
PLE_DIM = 256
EPS = 1e-6

LANES = 128
HEAD_PAD = LANES

TM_IN = 512
TP_POOL = 256
TM_OUT = 512
TR_ROUTE = 256
DEST_TILES = 16
BM_EXPERT = 512
VMEM_LIMIT = 56 * 1024 * 1024


def _rms(x, g):
    return x * lax.rsqrt(jnp.mean(x * x, axis=-1, keepdims=True) + EPS) * g


def _silu(x):
    h = 0.5 * x
    return h + h * jnp.tanh(h)


HALF = D_MODEL // 2
ROW_TILE = HALF // LANES
U32 = jnp.uint32


def _store_packed_rows(ref, v):
    m = v.shape[0]
    u = lax.bitcast_convert_type(v, U32)
    r = u + jnp.uint32(0x7FFF) + ((u >> 16) & jnp.uint32(1))
    w = (r[:, :HALF] >> 16) | (r[:, HALF:] & jnp.uint32(0xFFFF0000))
    for c in range(ROW_TILE):
        ref[pl.ds(c, m, stride=ROW_TILE), :] = w[:, c * LANES:(c + 1) * LANES]


def _load_packed_chunk(ref, c, m):
    w = ref[pl.ds(c, m, stride=ROW_TILE), :]
    lo = lax.bitcast_convert_type(w << 16, F32)
    hi = lax.bitcast_convert_type(w & jnp.uint32(0xFFFF0000), F32)
    return lo, hi


def _split3(v):
    hi = v.astype(BF16)
    r1 = v - hi.astype(F32)
    mid = r1.astype(BF16)
    lo = (r1 - mid.astype(F32)).astype(BF16)
    return hi, mid, lo


W_IN_COLS = D_SSD + D_CONV + D_POOL + HEAD_PAD


def _inproj_kernel(x_ref, g_ref, w_ref, z_ref, xbc_ref, pool_ref, dt_ref):
    hb = _rms(x_ref[...], g_ref[...]).astype(BF16)

    def mm(c0, n):
        return jnp.dot(hb, w_ref[:, c0:c0 + n], preferred_element_type=F32)

    cw = 512
    for c in range(0, D_SSD, cw):
        z_ref[:, c:c + cw] = mm(c, cw).astype(BF16)
    for c in range(0, D_CONV, cw):
        xbc_ref[:, c:c + cw] = mm(D_SSD + c, cw).astype(BF16)
    for c in range(0, D_POOL, cw):
        pool_ref[:, c:c + cw] = mm(D_SSD + D_CONV + c, cw).astype(BF16)
    dt_ref[...] = mm(D_SSD + D_CONV + D_POOL, HEAD_PAD)


def _inproj(x2d, g, w_cat):
    t = x2d.shape[0]
    tm = min(TM_IN, t)
    return pl.pallas_call(
        _inproj_kernel,
        grid=(t // tm,),
        in_specs=[
            pl.BlockSpec((tm, D_MODEL), lambda i: (i, 0)),
            pl.BlockSpec((1, D_MODEL), lambda i: (0, 0)),
            pl.BlockSpec((D_MODEL, W_IN_COLS), lambda i: (0, 0), pipeline_mode=pl.Buffered(1)),
        ],
        out_specs=[
            pl.BlockSpec((tm, D_SSD), lambda i: (i, 0)),
            pl.BlockSpec((tm, D_CONV), lambda i: (i, 0)),
            pl.BlockSpec((tm, D_POOL), lambda i: (i, 0)),
            pl.BlockSpec((tm, HEAD_PAD), lambda i: (i, 0)),
        ],
        out_shape=[
            jax.ShapeDtypeStruct((t, D_SSD), BF16),
            jax.ShapeDtypeStruct((t, D_CONV), BF16),
            jax.ShapeDtypeStruct((t, D_POOL), BF16),
            jax.ShapeDtypeStruct((t, HEAD_PAD), F32),
        ],
        compiler_params=pltpu.CompilerParams(
            dimension_semantics=("arbitrary",), vmem_limit_bytes=VMEM_LIMIT),
        name="inproj",
    )(x2d, g, w_cat)


def _ssd_kernel(xbc_ref, z_ref, dt_ref, convw_ref, convb_ref, dtb_ref, alog_ref, dskip_ref,
                ng_ref, tri_ref, expand_ref, y_ref, state_ref, hist_ref, xs_s, b_s, c_s):
    q = CHUNK

    @pl.when(pl.program_id(1) == 0)
    def _():
        state_ref[...] = jnp.zeros_like(state_ref)
        hist_ref[...] = jnp.zeros_like(hist_ref)

    cw = GROUP_W
    hr = hist_ref.shape[0]
    for ci in range(D_CONV // cw):
        cs = slice(ci * cw, (ci + 1) * cw)
        cur = xbc_ref[0, :, cs].astype(F32)
        xe = jnp.concatenate([hist_ref[:, cs], cur], axis=0)
        hist_ref[:, cs] = cur[q - hr:q, :]
        s1 = pltpu.roll(xe, 1, 0)
        u = convw_ref[3:4, cs] * xe + convw_ref[2:3, cs] * s1
        w = convw_ref[1:2, cs] * xe + convw_ref[0:1, cs] * s1
        acc = convb_ref[:, cs] + u + pltpu.roll(w, 2, 0)
        v = _silu(acc[hr:, :])
        if ci < D_SSD // cw:
            xs_s[ci] = v
        elif ci < (D_SSD + D_BC) // cw:
            j = ci - D_SSD // cw
            b_s[2 * j] = v[:, :D_STATE]
            b_s[2 * j + 1] = v[:, D_STATE:]
        else:
            j = ci - (D_SSD + D_BC) // cw
            c_s[2 * j] = v[:, :D_STATE].astype(BF16)
            c_s[2 * j + 1] = v[:, D_STATE:].astype(BF16)

    xdt = dt_ref[0] + dtb_ref[...]
    dt = jnp.maximum(xdt, 0.0) + jnp.log1p(jnp.exp(-jnp.abs(xdt)))
    a = dt * (-jnp.exp(alog_ref[...]))
    tri = tri_ref[...]
    a_hi, a_mid, a_lo = _split3(a)
    acs = (jnp.dot(tri, a_hi, preferred_element_type=F32)
           + jnp.dot(tri, a_mid, preferred_element_type=F32)
           + jnp.dot(tri, a_lo, preferred_element_type=F32))
    acs_t = acs.T
    dt_t = dt.T
    eacs = jnp.exp(acs)
    w_t = jnp.exp(acs_t[:, q - 1:q] - acs_t) * dt_t
    e_hi = eacs.astype(BF16)
    e_lo = (eacs - e_hi.astype(F32)).astype(BF16)

    row = lax.broadcasted_iota(I32, (q, q), 0)
    col = lax.broadcasted_iota(I32, (q, q), 1)
    causal = col <= row
    head_shift = SSD_HEAD_DIM.bit_length() - 1
    lane_head = jnp.right_shift(lax.broadcasted_iota(I32, (q, GROUP_W), 1), head_shift)
    head_mask = [jnp.where(lane_head == r, 1.0, 0.0).astype(BF16) for r in range(HEADS_PER_GROUP)]

    for g in range(SSD_GROUPS):
        gs = slice(g * GROUP_W, (g + 1) * GROUP_W)
        xs_g = xs_s[g]
        b_g = b_s[g]
        c_g = c_s[g]
        cb = lax.dot_general(c_g, b_g.astype(BF16), (((1,), (1,)), ((), ())),
                             preferred_element_type=F32)
        bt_g = b_g.T
        s_old = state_ref[g]
        y_off = jnp.dot(c_g, s_old.astype(BF16), preferred_element_type=F32)
        esc = (jnp.dot(e_hi, expand_ref[:, gs], preferred_element_type=F32)
               + jnp.dot(e_lo, expand_ref[:, gs], preferred_element_type=F32))
        xs_b = xs_g.astype(BF16)
        m_cat, bts_cat = [], []
        for r in range(HEADS_PER_GROUP):
            h = g * HEADS_PER_GROUP + r
            seg = acs[:, h:h + 1] - acs_t[h:h + 1, :]
            dec = jnp.where(causal, jnp.exp(seg), 0.0)
            m_cat.append((cb * dec * dt_t[h:h + 1, :]).astype(BF16))
            bts_cat.append((bt_g * w_t[h:h + 1, :]).astype(BF16))
        xs_bd = jnp.concatenate([xs_b * head_mask[r] for r in range(HEADS_PER_GROUP)], axis=0)
        y_diag = jnp.dot(jnp.concatenate(m_cat, axis=1), xs_bd, preferred_element_type=F32)
        ds = jnp.dot(jnp.concatenate(bts_cat, axis=1), xs_bd, preferred_element_type=F32)
        state_ref[g] = s_old * esc[q - 1:q, :] + ds
        y = y_diag + y_off * esc + dskip_ref[:, gs] * xs_g
        y = y * _silu(z_ref[0, :, gs].astype(F32))
        y_ref[0, :, gs] = _rms(y, ng_ref[:, gs]).astype(BF16)


def _ssd(xbc, z, dt, conv_w, conv_b, dtb, alog, dskip, ng, tri, expand):
    b, l, _ = xbc.shape
    nc = l // CHUNK
    cmap = lambda i, c: (0, 0)
    return pl.pallas_call(
        _ssd_kernel,
        grid=(b, nc),
        in_specs=[
            pl.BlockSpec((1, CHUNK, D_CONV), lambda i, c: (i, c, 0)),
            pl.BlockSpec((1, CHUNK, D_SSD), lambda i, c: (i, c, 0)),
            pl.BlockSpec((1, CHUNK, HEAD_PAD), lambda i, c: (i, c, 0)),
            pl.BlockSpec((CONV_WIDTH, D_CONV), cmap),
            pl.BlockSpec((1, D_CONV), cmap),
            pl.BlockSpec((1, HEAD_PAD), cmap),
            pl.BlockSpec((1, HEAD_PAD), cmap),
            pl.BlockSpec((1, D_SSD), cmap),
            pl.BlockSpec((1, D_SSD), cmap),
            pl.BlockSpec((CHUNK, CHUNK), cmap),
            pl.BlockSpec((HEAD_PAD, D_SSD), cmap),
        ],
        out_specs=pl.BlockSpec((1, CHUNK, D_SSD), lambda i, c: (i, c, 0)),
        out_shape=jax.ShapeDtypeStruct((b, l, D_SSD), BF16),
        scratch_shapes=[
            pltpu.VMEM((SSD_GROUPS, D_STATE, GROUP_W), F32),
            pltpu.VMEM((8, D_CONV), F32),
            pltpu.VMEM((SSD_GROUPS, CHUNK, GROUP_W), F32),
            pltpu.VMEM((SSD_GROUPS, CHUNK, D_STATE), F32),
            pltpu.VMEM((SSD_GROUPS, CHUNK, D_STATE), BF16),
        ],
        compiler_params=pltpu.CompilerParams(
            dimension_semantics=("arbitrary", "arbitrary"), vmem_limit_bytes=VMEM_LIMIT),
        name="ssd",
    )(xbc, z, dt, conv_w, conv_b, dtb, alog, dskip, ng, tri, expand)


def _pool_kernel(u_ref, band_ref, pw_ref, scale_ref, y_ref, hist_ref):
    tp = u_ref.shape[1]
    li = pl.program_id(1)

    @pl.when(li == 0)
    def _():
        hist_ref[0:POOL_HIST, :] = jnp.zeros((POOL_HIST, D_POOL), BF16)

    hist_ref[POOL_HIST:POOL_HIST + tp, :] = u_ref[0]
    pos = li * tp + lax.broadcasted_iota(I32, (tp, POOL_CH), 0)
    for gi, w in enumerate(POOL_WINDOWS):
        gs = slice(gi * POOL_CH, (gi + 1) * POOL_CH)
        win_sum = jnp.dot(band_ref[gi], hist_ref[:, gs], preferred_element_type=F32)
        cnt = jnp.minimum(pos + 1, w).astype(F32)
        pre = (win_sum / cnt - u_ref[0, :, gs].astype(F32)).astype(BF16)
        y = jnp.dot(pre, pw_ref[gi], preferred_element_type=F32) * scale_ref[:, gs]
        y_ref[0, :, gs] = y.astype(BF16)
    hist_ref[0:POOL_HIST, :] = hist_ref[tp:tp + POOL_HIST, :]


def _pool(u, band, pw, scale):
    b, l, _ = u.shape
    tp = band.shape[1]
    return pl.pallas_call(
        _pool_kernel,
        grid=(b, l // tp),
        in_specs=[
            pl.BlockSpec((1, tp, D_POOL), lambda i, j: (i, j, 0)),
            pl.BlockSpec((len(POOL_WINDOWS), tp, tp + POOL_HIST), lambda i, j: (0, 0, 0)),
            pl.BlockSpec((len(POOL_WINDOWS), POOL_CH, POOL_CH), lambda i, j: (0, 0, 0)),
            pl.BlockSpec((1, D_POOL), lambda i, j: (0, 0)),
        ],
        out_specs=pl.BlockSpec((1, tp, D_POOL), lambda i, j: (i, j, 0)),
        out_shape=jax.ShapeDtypeStruct((b, l, D_POOL), BF16),
        scratch_shapes=[pltpu.VMEM((POOL_HIST + tp, D_POOL), BF16)],
        compiler_params=pltpu.CompilerParams(
            dimension_semantics=("arbitrary", "arbitrary"), vmem_limit_bytes=VMEM_LIMIT),
        name="pool",
    )(u, band, pw, scale)


def _outproj_kernel(ys_ref, yp_ref, x_ref, ws_ref, wp_ref, g_ref, rwt_ref, rb_ref,
                    x1_ref, h_ref, lt_ref):
    acc = jnp.dot(ys_ref[...], ws_ref[...], preferred_element_type=F32)
    acc = acc + jnp.dot(yp_ref[...], wp_ref[...], preferred_element_type=F32)
    x1 = x_ref[...] + acc
    x1_ref[...] = x1
    h = _rms(x1, g_ref[...])
    _store_packed_rows(h_ref, h)
    lt = lax.dot_general(rwt_ref[...], h.astype(BF16), (((1,), (1,)), ((), ())),
                         preferred_element_type=F32)
    lt_ref[...] = lt + rb_ref[...]


def _outproj(y_ssd, y_pool, x2d, w_s, w_p, g, rwt, rb):
    t = x2d.shape[0]
    tm = min(TM_OUT, t)
    c2 = lambda i: (0, 0)
    return pl.pallas_call(
        _outproj_kernel,
        grid=(t // tm,),
        in_specs=[
            pl.BlockSpec((tm, D_SSD), lambda i: (i, 0)),
            pl.BlockSpec((tm, D_POOL), lambda i: (i, 0)),
            pl.BlockSpec((tm, D_MODEL), lambda i: (i, 0)),
            pl.BlockSpec((D_SSD, D_MODEL), c2),
            pl.BlockSpec((D_POOL, D_MODEL), c2),
            pl.BlockSpec((1, D_MODEL), c2),
            pl.BlockSpec((N_EXPERTS, D_MODEL), c2),
            pl.BlockSpec((N_EXPERTS, 1), c2),
        ],
        out_specs=[
            pl.BlockSpec((tm, D_MODEL), lambda i: (i, 0)),
            pl.BlockSpec((tm * ROW_TILE, LANES), lambda i: (i, 0)),
            pl.BlockSpec((N_EXPERTS, tm), lambda i: (0, i)),
        ],
        out_shape=[
            jax.ShapeDtypeStruct((t, D_MODEL), F32),
            jax.ShapeDtypeStruct((t * ROW_TILE, LANES), U32),
            jax.ShapeDtypeStruct((N_EXPERTS, t), F32),
        ],
        compiler_params=pltpu.CompilerParams(
            dimension_semantics=("arbitrary",), vmem_limit_bytes=VMEM_LIMIT),
        name="outproj",
    )(y_ssd, y_pool, x2d, w_s, w_p, g, rwt, rb)


def _route_kernel(lt_ref, utri_ref, idx_ref, gate_ref, rank_ref, cnt_ref, carry_ref):
    tr = lt_ref.shape[1]

    @pl.when(pl.program_id(0) == 0)
    def _():
        carry_ref[...] = jnp.zeros_like(carry_ref)

    l = lt_ref[...]
    eidx = lax.broadcasted_iota(I32, (N_EXPERTS, tr), 0).astype(F32)
    sels, vals, idxs = [], [], []
    for _ in range(TOP_K):
        m = jnp.max(l, axis=0, keepdims=True)
        ik = jnp.min(jnp.where(l == m, eidx, float(N_EXPERTS)), axis=0, keepdims=True)
        sel = eidx == ik
        l = jnp.where(sel, -jnp.inf, l)
        sels.append(sel)
        vals.append(m)
        idxs.append(ik)
    es = [jnp.exp(v - vals[0]) for v in vals]
    den = es[0] + es[1] + es[2] + es[3]
    multi = jnp.zeros((N_EXPERTS, tr), F32)
    for sel in sels:
        multi = multi + jnp.where(sel, 1.0, 0.0)
    incl = jnp.dot(multi.astype(BF16), utri_ref[...], preferred_element_type=F32)
    excl = incl - multi + carry_ref[:, 0:1]
    for k in range(TOP_K):
        idx_ref[0, k:k + 1, :] = idxs[k].astype(I32)
        gate_ref[0, k:k + 1, :] = es[k] / den
        rk = jnp.sum(jnp.where(sels[k], excl, 0.0), axis=0, keepdims=True)
        rank_ref[0, k:k + 1, :] = rk.astype(I32)
    carry_ref[...] = carry_ref[...] + jnp.sum(multi, axis=1, keepdims=True)
    cnt_ref[...] = carry_ref[...]


def _route(lt, utri):
    t = lt.shape[1]
    tr = utri.shape[0]
    nt = t // tr
    o3 = lambda i: (i, 0, 0)
    return pl.pallas_call(
        _route_kernel,
        grid=(nt,),
        in_specs=[
            pl.BlockSpec((N_EXPERTS, tr), lambda i: (0, i)),
            pl.BlockSpec((tr, tr), lambda i: (0, 0)),
        ],
        out_specs=[
            pl.BlockSpec((1, TOP_K, tr), o3),
            pl.BlockSpec((1, TOP_K, tr), o3),
            pl.BlockSpec((1, TOP_K, tr), o3),
            pl.BlockSpec((N_EXPERTS, LANES), lambda i: (0, 0)),
        ],
        out_shape=[
            jax.ShapeDtypeStruct((nt, TOP_K, tr), I32),
            jax.ShapeDtypeStruct((nt, TOP_K, tr), F32),
            jax.ShapeDtypeStruct((nt, TOP_K, tr), I32),
            jax.ShapeDtypeStruct((N_EXPERTS, LANES), F32),
        ],
        scratch_shapes=[pltpu.VMEM((N_EXPERTS, LANES), F32)],
        compiler_params=pltpu.CompilerParams(
            dimension_semantics=("arbitrary",), vmem_limit_bytes=VMEM_LIMIT),
        name="route",
    )(lt, utri)


def _dest_kernel(pstart_ref, idx_ref, rank_ref, dest_ref):
    idx = idx_ref[...]
    base = jnp.zeros_like(idx)
    for e in range(N_EXPERTS):
        base = jnp.where(idx == e, pstart_ref[e], base)
    dest_ref[...] = base + rank_ref[...]


def _dest(pstart, idx, rank):
    nt, _, tr = idx.shape
    tb = min(DEST_TILES, nt)
    o3 = lambda i, ps: (i, 0, 0)
    return pl.pallas_call(
        _dest_kernel,
        grid_spec=pltpu.PrefetchScalarGridSpec(
            num_scalar_prefetch=1,
            grid=(nt // tb,),
            in_specs=[pl.BlockSpec((tb, TOP_K, tr), o3), pl.BlockSpec((tb, TOP_K, tr), o3)],
            out_specs=pl.BlockSpec((tb, TOP_K, tr), o3),
        ),
        out_shape=jax.ShapeDtypeStruct((nt, TOP_K, tr), I32),
        compiler_params=pltpu.CompilerParams(dimension_semantics=("arbitrary",)),
        name="dest",
    )(pstart, idx, rank)


def _row_tile(r):
    return pl.ds(pl.multiple_of(r * ROW_TILE, ROW_TILE), ROW_TILE)


def _dispatch_kernel(cnt_ref, pad_ref, pstart_ref, nu_ref, dest_hbm, h_ref, xs_hbm, dsm, zrow, sem_d,
                     sem, sem_z):
    i = pl.program_id(0)
    w = dsm.shape[0] // 2
    tr = w // TOP_K
    zrows = zrow.shape[0]
    nb = xs_hbm.shape[0] // zrows
    slot = lax.rem(i, 2)

    def dest_copy(step, s):
        return pltpu.make_async_copy(dest_hbm.at[step], dsm.at[pl.ds(pl.multiple_of(s * w, w), w)],
                                     sem_d)

    def zero_copy(r):
        return pltpu.make_async_copy(zrow.at[pl.ds(0, ROW_TILE), :], xs_hbm.at[_row_tile(r), :], sem)

    def zero_block_copy(blk):
        rows = pl.ds(pl.multiple_of(blk * zrows, zrows), zrows)
        return pltpu.make_async_copy(zrow, xs_hbm.at[rows, :], sem_z)

    @pl.when(i == 0)
    def _():
        dest_copy(0, 0).start()
        zrow[...] = jnp.zeros_like(zrow)

        def zstart(blk, c):
            zero_block_copy(blk).start()
            return c

        def zwait(blk, c):
            zero_block_copy(blk).wait()
            return c

        lax.fori_loop(nu_ref[0], nb, zstart, 0)
        lax.fori_loop(nu_ref[0], nb, zwait, 0)
        for e in range(N_EXPERTS):
            lo, hi, base = cnt_ref[e], pad_ref[e], pstart_ref[e]

            def start(j, c, base=base):
                zero_copy(base + j).start()
                return c

            def wait(j, c, base=base):
                zero_copy(base + j).wait()
                return c

            lax.fori_loop(lo, hi, start, 0)
            lax.fori_loop(lo, hi, wait, 0)

    dest_copy(i, slot).wait()

    @pl.when(i + 1 < pl.num_programs(0))
    def _():
        dest_copy(i + 1, 1 - slot).start()

    def row_copy(t, k):
        d = dsm[slot * w + k * tr + t]
        return pltpu.make_async_copy(h_ref.at[_row_tile(t), :], xs_hbm.at[_row_tile(d), :], sem)

    def start(t, c):
        for k in range(TOP_K):
            row_copy(t, k).start(priority=k % 2)
        return c

    def wait(t, c):
        for k in range(TOP_K):
            row_copy(t, k).wait()
        return c

    lax.fori_loop(0, tr, start, 0, unroll=2)
    lax.fori_loop(0, tr, wait, 0, unroll=8)


def _dispatch(counts, padded, pstart, n_used, dest2d, h, n_slots):
    nt, w = dest2d.shape
    tr = w // TOP_K
    return pl.pallas_call(
        _dispatch_kernel,
        grid_spec=pltpu.PrefetchScalarGridSpec(
            num_scalar_prefetch=4,
            grid=(nt,),
            in_specs=[
                pl.BlockSpec(memory_space=pl.ANY),
                pl.BlockSpec((tr * ROW_TILE, LANES), lambda i, *_: (i, 0)),
            ],
            out_specs=pl.BlockSpec(memory_space=pl.ANY),
            scratch_shapes=[
                pltpu.SMEM((2 * w,), I32),
                pltpu.VMEM((BM_EXPERT * ROW_TILE, LANES), U32),
                pltpu.SemaphoreType.DMA,
                pltpu.SemaphoreType.DMA,
                pltpu.SemaphoreType.DMA,
            ],
        ),
        out_shape=jax.ShapeDtypeStruct((n_slots * ROW_TILE, LANES), U32),
        compiler_params=pltpu.CompilerParams(dimension_semantics=("arbitrary",)),
        name="dispatch",
    )(counts, padded, pstart, n_used, dest2d, h)


def _expert_kernel(be_ref, nu_ref, xs_ref, wgu_ref, bgu_ref, wd_ref, bd_ref, ys_ref, xb_s, wgu_s,
                   wd_s):
    i = pl.program_id(0)
    bm = xb_s.shape[0]

    @pl.when((i == 0) | (be_ref[i] != be_ref[jnp.maximum(i - 1, 0)]))
    def _():
        rc = 128
        for r in range(0, D_MODEL, rc):
            wgu_s[r:r + rc, :] = wgu_ref[0, r:r + rc, :].astype(BF16)
        for r in range(0, D_FF, rc):
            wd_s[r:r + rc, :] = wd_ref[0, r:r + rc, :].astype(BF16)

    @pl.when(i < nu_ref[0])
    def _():
        for c in range(ROW_TILE):
            lo, hi = _load_packed_chunk(xs_ref, c, bm)
            xb_s[:, c * LANES:(c + 1) * LANES] = lo.astype(BF16)
            xb_s[:, HALF + c * LANES:HALF + (c + 1) * LANES] = hi.astype(BF16)
        gu = jnp.dot(xb_s[...], wgu_s[...], preferred_element_type=F32) + bgu_ref[0]
        gt = jnp.minimum(gu[:, :D_FF], SWIGLU_LIMIT)
        up = jnp.clip(gu[:, D_FF:], -SWIGLU_LIMIT, SWIGLU_LIMIT)
        hg = 0.5 * gt
        act = (up + 1.0) * (hg + hg * jnp.tanh(SWIGLU_ALPHA * hg))
        y = jnp.dot(act.astype(BF16), wd_s[...], preferred_element_type=F32) + bd_ref[0]
        _store_packed_rows(ys_ref, y)

    @pl.when(i >= nu_ref[0])
    def _():
        ys_ref[...] = jnp.zeros_like(ys_ref)


def _experts(block_e, n_used, xs, wgu, bgu, wd, bd):
    bm = BM_EXPERT
    nb = xs.shape[0] // (bm * ROW_TILE)
    row = lambda i, be, nu: (jnp.minimum(i, nu[0] - 1), 0)
    orow = lambda i, be, nu: (i, 0)
    wsel = lambda i, be, nu: (be[i], 0, 0)
    return pl.pallas_call(
        _expert_kernel,
        grid_spec=pltpu.PrefetchScalarGridSpec(
            num_scalar_prefetch=2,
            grid=(nb,),
            in_specs=[
                pl.BlockSpec((bm * ROW_TILE, LANES), row),
                pl.BlockSpec((1, D_MODEL, 2 * D_FF), wsel),
                pl.BlockSpec((1, 1, 2 * D_FF), wsel),
                pl.BlockSpec((1, D_FF, D_MODEL), wsel),
                pl.BlockSpec((1, 1, D_MODEL), wsel),
            ],
            out_specs=pl.BlockSpec((bm * ROW_TILE, LANES), orow),
            scratch_shapes=[
                pltpu.VMEM((bm, D_MODEL), BF16),
                pltpu.VMEM((D_MODEL, 2 * D_FF), BF16),
                pltpu.VMEM((D_FF, D_MODEL), BF16),
            ],
        ),
        out_shape=jax.ShapeDtypeStruct(xs.shape, U32),
        compiler_params=pltpu.CompilerParams(
            dimension_semantics=("arbitrary",), vmem_limit_bytes=VMEM_LIMIT),
        name="experts",
    )(block_e, n_used, xs, wgu, bgu, wd, bd)


def _combine_kernel(dest_hbm, ys_hbm, x1_ref, gate_ref, p_ref, gng_ref, wg_ref, wp_ref, png_ref,
                    fng_ref, o_ref, dsm, ybuf, sem_d, sem):
    i = pl.program_id(0)
    tr = x1_ref.shape[0]
    w = TOP_K * tr
    slot = lax.rem(i, 2)

    def dest_copy(step, s):
        return pltpu.make_async_copy(dest_hbm.at[step], dsm.at[pl.ds(pl.multiple_of(s * w, w), w)],
                                     sem_d)

    def row_copy(s, t, k):
        d = dsm[s * w + k * tr + t]
        return pltpu.make_async_copy(ys_hbm.at[_row_tile(d), :], ybuf.at[s, k, _row_tile(t), :],
                                     sem.at[s])

    def gather_tile(step, s):
        cp = dest_copy(step, s)
        cp.start()
        cp.wait()

        def start(t, c):
            for k in range(TOP_K):
                row_copy(s, t, k).start(priority=k % 2)
            return c

        lax.fori_loop(0, tr, start, 0, unroll=2)

    @pl.when(i == 0)
    def _():
        gather_tile(0, 0)

    @pl.when(i + 1 < pl.num_programs(0))
    def _():
        gather_tile(i + 1, 1 - slot)

    def wait(t, c):
        for k in range(TOP_K):
            row_copy(slot, t, k).wait()
        return c

    lax.fori_loop(0, tr, wait, 0, unroll=8)

    lo_chunks, hi_chunks = [], []
    for c in range(ROW_TILE):
        acc_lo = x1_ref[:, c * LANES:(c + 1) * LANES]
        acc_hi = x1_ref[:, HALF + c * LANES:HALF + (c + 1) * LANES]
        for k in range(TOP_K):
            lo, hi = _load_packed_chunk(ybuf.at[slot, k], c, tr)
            g = gate_ref[:, k:k + 1]
            acc_lo = acc_lo + g * lo
            acc_hi = acc_hi + g * hi
        lo_chunks.append(acc_lo)
        hi_chunks.append(acc_hi)
    x2 = jnp.concatenate(lo_chunks + hi_chunks, axis=1)
    gate = jnp.dot(_rms(x2, gng_ref[...]).astype(BF16), wg_ref[...], preferred_element_type=F32)
    gate = 1.0 / (1.0 + jnp.exp(-gate))
    e = jnp.dot(p_ref[...].astype(BF16), wp_ref[...], preferred_element_type=F32)
    x3 = x2 + _rms(e, png_ref[...]) * gate
    o_ref[...] = _rms(x3, fng_ref[...])


def _combine(dest2d, ys, x1, gate_t, p2d, gng, wg, wp, png, fng):
    t = x1.shape[0]
    nt, w = dest2d.shape
    tr = w // TOP_K
    c2 = lambda i: (0, 0)
    return pl.pallas_call(
        _combine_kernel,
        grid=(nt,),
        in_specs=[
            pl.BlockSpec(memory_space=pl.ANY),
            pl.BlockSpec(memory_space=pl.ANY),
            pl.BlockSpec((tr, D_MODEL), lambda i: (i, 0)),
            pl.BlockSpec((tr, TOP_K), lambda i: (i, 0)),
            pl.BlockSpec((tr, PLE_DIM), lambda i: (i, 0)),
            pl.BlockSpec((1, D_MODEL), c2),
            pl.BlockSpec((D_MODEL, D_MODEL), c2),
            pl.BlockSpec((PLE_DIM, D_MODEL), c2),
            pl.BlockSpec((1, D_MODEL), c2),
            pl.BlockSpec((1, D_MODEL), c2),
        ],
        out_specs=pl.BlockSpec((tr, D_MODEL), lambda i: (i, 0)),
        out_shape=jax.ShapeDtypeStruct((t, D_MODEL), F32),
        scratch_shapes=[
            pltpu.SMEM((2 * w,), I32),
            pltpu.VMEM((2, TOP_K, tr * ROW_TILE, LANES), U32),
            pltpu.SemaphoreType.DMA,
            pltpu.SemaphoreType.DMA((2,)),
        ],
        compiler_params=pltpu.CompilerParams(
            dimension_semantics=("arbitrary",), vmem_limit_bytes=VMEM_LIMIT),
        name="combine",
    )(dest2d, ys, x1, gate_t, p2d, gng, wg, wp, png, fng)


def _pad_heads(v):
    return jnp.pad(v.astype(F32), (0, HEAD_PAD - SSD_HEADS)).reshape(1, HEAD_PAD)


def _layer(x, p, mix_norm_g, w_in, conv_w, conv_b, dt_bias, a_log, d_skip, ssd_norm_g, pool_w,
           pool_scale, w_out, ffn_norm_g, router_w, router_b, w_gate_up, b_gate_up, w_down,
           b_down, ple_gate_norm_g, w_ple_gate, w_ple_proj, ple_norm_g, final_g):
    b, l, _ = x.shape
    t = b * l
    x2d = x.reshape(t, D_MODEL)
    row = lambda v: v.reshape(1, -1).astype(F32)

    c_dt = D_SSD + D_CONV
    c_pool = c_dt + SSD_HEADS
    w_cat = jnp.concatenate(
        [w_in[:, :c_dt], w_in[:, c_pool:], w_in[:, c_dt:c_pool],
         jnp.zeros((D_MODEL, HEAD_PAD - SSD_HEADS), w_in.dtype)], axis=1).astype(BF16)
    ii = jnp.arange(CHUNK)
    tri = (ii[None, :] <= ii[:, None]).astype(BF16)
    tp = min(TP_POOL, l)
    rr = jnp.arange(tp)[:, None] + POOL_HIST
    jj = jnp.arange(tp + POOL_HIST)[None, :]
    band = jnp.stack([((jj <= rr) & (jj > rr - w)) for w in POOL_WINDOWS]).astype(BF16)
    tr = min(TR_ROUTE, t)
    ri = jnp.arange(tr)
    utri = (ri[:, None] <= ri[None, :]).astype(BF16)

    head_of_col = jnp.arange(D_SSD) // SSD_HEAD_DIM
    expand = (jnp.arange(HEAD_PAD)[:, None] == head_of_col[None, :]).astype(BF16)

    z, xbc, pool_in, dt = _inproj(x2d, row(mix_norm_g), w_cat)
    y_ssd = _ssd(xbc.reshape(b, l, D_CONV), z.reshape(b, l, D_SSD), dt.reshape(b, l, HEAD_PAD),
                 conv_w.astype(F32), row(conv_b), _pad_heads(dt_bias), _pad_heads(a_log),
                 row(jnp.repeat(d_skip, SSD_HEAD_DIM)), row(ssd_norm_g), tri, expand)
    y_pool = _pool(pool_in.reshape(b, l, D_POOL), band, pool_w.astype(BF16), row(pool_scale))
    x1, h, lt = _outproj(y_ssd.reshape(t, D_SSD), y_pool.reshape(t, D_POOL), x2d,
                         w_out[:D_SSD].astype(BF16), w_out[D_SSD:].astype(BF16), row(ffn_norm_g),
                         router_w.T.astype(BF16), router_b.reshape(N_EXPERTS, 1).astype(F32))

    idx, gate, rank, cnt = _route(lt, utri)
    counts = cnt[:, 0].astype(I32)
    bm = BM_EXPERT
    padded = ((counts + bm - 1) // bm) * bm
    pend = jnp.cumsum(padded)
    pstart = (pend - padded).astype(I32)
    nb = (t * TOP_K) // bm + N_EXPERTS
    n_used = (pend[-1] // bm).astype(I32)
    blk = jnp.arange(nb, dtype=I32)
    block_e = jnp.sum((pend[None, :] <= (blk * bm)[:, None]).astype(I32), axis=1)
    block_e = jnp.minimum(block_e, N_EXPERTS - 1)
    last_e = jnp.sum((pend <= (n_used - 1) * bm).astype(I32))
    block_e = jnp.where(blk < n_used, block_e, jnp.minimum(last_e, N_EXPERTS - 1)).astype(I32)

    dest = _dest(pstart, idx, rank)
    nt = t // tr
    dest2d = dest.reshape(nt, TOP_K * tr)
    xs = _dispatch(counts, padded.astype(I32), pstart, n_used.reshape(1), dest2d, h, nb * bm)
    ys = _experts(block_e, n_used.reshape(1), xs, w_gate_up.astype(F32),
                  b_gate_up.reshape(N_EXPERTS, 1, 2 * D_FF).astype(F32), w_down.astype(F32),
                  b_down.reshape(N_EXPERTS, 1, D_MODEL).astype(F32))
    gate_t = gate.transpose(0, 2, 1).reshape(t, TOP_K)
    out = _combine(dest2d, ys, x1, gate_t, p.reshape(t, PLE_DIM), row(ple_gate_norm_g),
                   w_ple_gate.astype(BF16), w_ple_proj.astype(BF16), row(ple_norm_g), row(final_g))
    return out.reshape(b, l, D_MODEL)


def kernel(x, p, mix_norm_g, w_in, conv_w, conv_b, dt_bias, a_log, d_skip, ssd_norm_g, pool_w, pool_scale, w_out, ffn_norm_g, router_w, router_b, w_gate_up, b_gate_up, w_down, b_down, ple_gate_norm_g, w_ple_gate, w_ple_proj, ple_norm_g, final_norm_g):
    assert x.shape[-1] == D_MODEL and mix_norm_g.shape[0] == 1
    i = 0
    return _layer(x, p[i], mix_norm_g[i], w_in[i], conv_w[i], conv_b[i], dt_bias[i], a_log[i],
                  d_skip[i], ssd_norm_g[i], pool_w[i], pool_scale[i], w_out[i], ffn_norm_g[i],
                  router_w[i], router_b[i], w_gate_up[i], b_gate_up[i], w_down[i], b_down[i],
                  ple_gate_norm_g[i], w_ple_gate[i], w_ple_proj[i], ple_norm_g[i], final_norm_g)
```

```python
import functools

import jax
import jax.numpy as jnp
from jax import lax
from jax.experimental import pallas as pl
from jax.experimental.pallas import tpu as pltpu

F32 = jnp.float32
BF16 = jnp.bfloat16
I32 = jnp.int32

D_MODEL = 1024
SSD_HEADS = 32
SSD_HEAD_DIM = 64
D_SSD = SSD_HEADS * SSD_HEAD_DIM
SSD_GROUPS = 8
HEADS_PER_GROUP = SSD_HEADS // SSD_GROUPS
GROUP_W = HEADS_PER_GROUP * SSD_HEAD_DIM
D_STATE = 128
CONV_WIDTH = 4
CHUNK = 128
D_BC = SSD_GROUPS * D_STATE
D_CONV = D_SSD + 2 * D_BC
POOL_WINDOWS = (2, 4, 8, 16)
POOL_CH = 256
D_POOL = len(POOL_WINDOWS) * POOL_CH
POOL_HIST = 128
N_EXPERTS = 32
TOP_K = 4
D_FF = 1024
SWIGLU_LIMIT = 7.0
SWIGLU_ALPHA = 1.702
PLE_DIM = 256
EPS = 1e-6

LANES = 128
HEAD_PAD = LANES

TM_IN = 512
TP_POOL = 256
TM_OUT = 512
TR_ROUTE = 256
DEST_TILES = 16
BM_EXPERT = 512
VMEM_LIMIT = 56 * 1024 * 1024


def _rms(x, g):
    return x * lax.rsqrt(jnp.mean(x * x, axis=-1, keepdims=True) + EPS) * g


def _silu(x):
    h = 0.5 * x
    return h + h * jnp.tanh(h)


HALF = D_MODEL // 2
ROW_TILE = HALF // LANES
U32 = jnp.uint32


def _store_packed_rows(ref, v):
    m = v.shape[0]
    u = lax.bitcast_convert_type(v, U32)
    r = u + jnp.uint32(0x7FFF) + ((u >> 16) & jnp.uint32(1))
    w = (r[:, :HALF] >> 16) | (r[:, HALF:] & jnp.uint32(0xFFFF0000))
    for c in range(ROW_TILE):
        ref[pl.ds(c, m, stride=ROW_TILE), :] = w[:, c * LANES:(c + 1) * LANES]


def _load_packed_chunk(ref, c, m, row_stride=ROW_TILE):
    w = ref[pl.ds(c, m, stride=row_stride), :]
    lo = lax.bitcast_convert_type(w << 16, F32)
    hi = lax.bitcast_convert_type(w & jnp.uint32(0xFFFF0000), F32)
    return lo, hi


def _split3(v):
    hi = v.astype(BF16)
    r1 = v - hi.astype(F32)
    mid = r1.astype(BF16)
    lo = (r1 - mid.astype(F32)).astype(BF16)
    return hi, mid, lo


W_IN_COLS = D_SSD + D_CONV + D_POOL + HEAD_PAD


def _inproj_kernel(x_ref, g_ref, w_ref, z_ref, xbc_ref, pool_ref, dt_ref):
    hb = _rms(x_ref[...], g_ref[...]).astype(BF16)

    def mm(c0, n):
        return jnp.dot(hb, w_ref[:, c0:c0 + n], preferred_element_type=F32)

    cw = 512
    for c in range(0, D_SSD, cw):
        z_ref[:, c:c + cw] = mm(c, cw).astype(BF16)
    for c in range(0, D_CONV, cw):
        xbc_ref[:, c:c + cw] = mm(D_SSD + c, cw).astype(BF16)
    for c in range(0, D_POOL, cw):
        pool_ref[:, c:c + cw] = mm(D_SSD + D_CONV + c, cw).astype(BF16)
    dt_ref[...] = mm(D_SSD + D_CONV + D_POOL, HEAD_PAD)


def _inproj(x2d, g, w_cat):
    t = x2d.shape[0]
    tm = min(TM_IN, t)
    return pl.pallas_call(
        _inproj_kernel,
        grid=(t // tm,),
        in_specs=[
            pl.BlockSpec((tm, D_MODEL), lambda i: (i, 0)),
            pl.BlockSpec((1, D_MODEL), lambda i: (0, 0)),
            pl.BlockSpec((D_MODEL, W_IN_COLS), lambda i: (0, 0), pipeline_mode=pl.Buffered(1)),
        ],
        out_specs=[
            pl.BlockSpec((tm, D_SSD), lambda i: (i, 0)),
            pl.BlockSpec((tm, D_CONV), lambda i: (i, 0)),
            pl.BlockSpec((tm, D_POOL), lambda i: (i, 0)),
            pl.BlockSpec((tm, HEAD_PAD), lambda i: (i, 0)),
        ],
        out_shape=[
            jax.ShapeDtypeStruct((t, D_SSD), BF16),
            jax.ShapeDtypeStruct((t, D_CONV), BF16),
            jax.ShapeDtypeStruct((t, D_POOL), BF16),
            jax.ShapeDtypeStruct((t, HEAD_PAD), F32),
        ],
        compiler_params=pltpu.CompilerParams(
            dimension_semantics=("arbitrary",), vmem_limit_bytes=VMEM_LIMIT),
        name="inproj",
    )(x2d, g, w_cat)


def _ssd_kernel(xbc_ref, z_ref, dt_ref, convw_ref, convb_ref, dtb_ref, alog_ref, dskip_ref,
                ng_ref, tri_ref, expand_ref, y_ref, state_ref, hist_ref, xs_s, b_s, c_s):
    q = CHUNK

    @pl.when(pl.program_id(1) == 0)
    def _():
        state_ref[...] = jnp.zeros_like(state_ref)
        hist_ref[...] = jnp.zeros_like(hist_ref)

    cw = GROUP_W
    hr = hist_ref.shape[0]
    for ci in range(D_CONV // cw):
        cs = slice(ci * cw, (ci + 1) * cw)
        cur = xbc_ref[0, :, cs].astype(F32)
        xe = jnp.concatenate([hist_ref[:, cs], cur], axis=0)
        hist_ref[:, cs] = cur[q - hr:q, :]
        s1 = pltpu.roll(xe, 1, 0)
        u = convw_ref[3:4, cs] * xe + convw_ref[2:3, cs] * s1
        w = convw_ref[1:2, cs] * xe + convw_ref[0:1, cs] * s1
        acc = convb_ref[:, cs] + u + pltpu.roll(w, 2, 0)
        v = _silu(acc[hr:, :])
        if ci < D_SSD // cw:
            xs_s[ci] = v
        elif ci < (D_SSD + D_BC) // cw:
            j = ci - D_SSD // cw
            b_s[2 * j] = v[:, :D_STATE]
            b_s[2 * j + 1] = v[:, D_STATE:]
        else:
            j = ci - (D_SSD + D_BC) // cw
            c_s[2 * j] = v[:, :D_STATE].astype(BF16)
            c_s[2 * j + 1] = v[:, D_STATE:].astype(BF16)

    xdt = dt_ref[0] + dtb_ref[...]
    dt = jnp.maximum(xdt, 0.0) + jnp.log1p(jnp.exp(-jnp.abs(xdt)))
    a = dt * (-jnp.exp(alog_ref[...]))
    tri = tri_ref[...]
    a_hi, a_mid, a_lo = _split3(a)
    acs = (jnp.dot(tri, a_hi, preferred_element_type=F32)
           + jnp.dot(tri, a_mid, preferred_element_type=F32)
           + jnp.dot(tri, a_lo, preferred_element_type=F32))
    acs_t = acs.T
    dt_t = dt.T
    eacs = jnp.exp(acs)
    w_t = jnp.exp(acs_t[:, q - 1:q] - acs_t) * dt_t
    e_hi = eacs.astype(BF16)
    e_lo = (eacs - e_hi.astype(F32)).astype(BF16)

    row = lax.broadcasted_iota(I32, (q, q), 0)
    col = lax.broadcasted_iota(I32, (q, q), 1)
    causal = col <= row
    head_shift = SSD_HEAD_DIM.bit_length() - 1
    lane_head = jnp.right_shift(lax.broadcasted_iota(I32, (q, GROUP_W), 1), head_shift)
    head_mask = [jnp.where(lane_head == r, 1.0, 0.0).astype(BF16) for r in range(HEADS_PER_GROUP)]

    for g in range(SSD_GROUPS):
        gs = slice(g * GROUP_W, (g + 1) * GROUP_W)
        xs_g = xs_s[g]
        b_g = b_s[g]
        c_g = c_s[g]
        cb = lax.dot_general(c_g, b_g.astype(BF16), (((1,), (1,)), ((), ())),
                             preferred_element_type=F32)
        bt_g = b_g.T
        s_old = state_ref[g]
        y_off = jnp.dot(c_g, s_old.astype(BF16), preferred_element_type=F32)
        esc = (jnp.dot(e_hi, expand_ref[:, gs], preferred_element_type=F32)
               + jnp.dot(e_lo, expand_ref[:, gs], preferred_element_type=F32))
        xs_b = xs_g.astype(BF16)
        m_cat, bts_cat = [], []
        for r in range(HEADS_PER_GROUP):
            h = g * HEADS_PER_GROUP + r
            seg = acs[:, h:h + 1] - acs_t[h:h + 1, :]
            dec = jnp.where(causal, jnp.exp(seg), 0.0)
            m_cat.append((cb * dec * dt_t[h:h + 1, :]).astype(BF16))
            bts_cat.append((bt_g * w_t[h:h + 1, :]).astype(BF16))
        xs_bd = jnp.concatenate([xs_b * head_mask[r] for r in range(HEADS_PER_GROUP)], axis=0)
        y_diag = jnp.dot(jnp.concatenate(m_cat, axis=1), xs_bd, preferred_element_type=F32)
        ds = jnp.dot(jnp.concatenate(bts_cat, axis=1), xs_bd, preferred_element_type=F32)
        state_ref[g] = s_old * esc[q - 1:q, :] + ds
        y = y_diag + y_off * esc + dskip_ref[:, gs] * xs_g
        y = y * _silu(z_ref[0, :, gs].astype(F32))
        y_ref[0, :, gs] = _rms(y, ng_ref[:, gs]).astype(BF16)


def _ssd(xbc, z, dt, conv_w, conv_b, dtb, alog, dskip, ng, tri, expand):
    b, l, _ = xbc.shape
    nc = l // CHUNK
    cmap = lambda i, c: (0, 0)
    return pl.pallas_call(
        _ssd_kernel,
        grid=(b, nc),
        in_specs=[
            pl.BlockSpec((1, CHUNK, D_CONV), lambda i, c: (i, c, 0)),
            pl.BlockSpec((1, CHUNK, D_SSD), lambda i, c: (i, c, 0)),
            pl.BlockSpec((1, CHUNK, HEAD_PAD), lambda i, c: (i, c, 0)),
            pl.BlockSpec((CONV_WIDTH, D_CONV), cmap),
            pl.BlockSpec((1, D_CONV), cmap),
            pl.BlockSpec((1, HEAD_PAD), cmap),
            pl.BlockSpec((1, HEAD_PAD), cmap),
            pl.BlockSpec((1, D_SSD), cmap),
            pl.BlockSpec((1, D_SSD), cmap),
            pl.BlockSpec((CHUNK, CHUNK), cmap),
            pl.BlockSpec((HEAD_PAD, D_SSD), cmap),
        ],
        out_specs=pl.BlockSpec((1, CHUNK, D_SSD), lambda i, c: (i, c, 0)),
        out_shape=jax.ShapeDtypeStruct((b, l, D_SSD), BF16),
        scratch_shapes=[
            pltpu.VMEM((SSD_GROUPS, D_STATE, GROUP_W), F32),
            pltpu.VMEM((8, D_CONV), F32),
            pltpu.VMEM((SSD_GROUPS, CHUNK, GROUP_W), F32),
            pltpu.VMEM((SSD_GROUPS, CHUNK, D_STATE), F32),
            pltpu.VMEM((SSD_GROUPS, CHUNK, D_STATE), BF16),
        ],
        compiler_params=pltpu.CompilerParams(
            dimension_semantics=("arbitrary", "arbitrary"), vmem_limit_bytes=VMEM_LIMIT),
        name="ssd",
    )(xbc, z, dt, conv_w, conv_b, dtb, alog, dskip, ng, tri, expand)


def _pool_kernel(u_ref, band_ref, pw_ref, scale_ref, y_ref, hist_ref):
    tp = u_ref.shape[1]
    li = pl.program_id(1)

    @pl.when(li == 0)
    def _():
        hist_ref[0:POOL_HIST, :] = jnp.zeros((POOL_HIST, D_POOL), BF16)

    hist_ref[POOL_HIST:POOL_HIST + tp, :] = u_ref[0]
    pos = li * tp + lax.broadcasted_iota(I32, (tp, POOL_CH), 0)
    for gi, w in enumerate(POOL_WINDOWS):
        gs = slice(gi * POOL_CH, (gi + 1) * POOL_CH)
        win_sum = jnp.dot(band_ref[gi], hist_ref[:, gs], preferred_element_type=F32)
        cnt = jnp.minimum(pos + 1, w).astype(F32)
        pre = (win_sum / cnt - u_ref[0, :, gs].astype(F32)).astype(BF16)
        y = jnp.dot(pre, pw_ref[gi], preferred_element_type=F32) * scale_ref[:, gs]
        y_ref[0, :, gs] = y.astype(BF16)
    hist_ref[0:POOL_HIST, :] = hist_ref[tp:tp + POOL_HIST, :]


def _pool(u, band, pw, scale):
    b, l, _ = u.shape
    tp = band.shape[1]
    return pl.pallas_call(
        _pool_kernel,
        grid=(b, l // tp),
        in_specs=[
            pl.BlockSpec((1, tp, D_POOL), lambda i, j: (i, j, 0)),
            pl.BlockSpec((len(POOL_WINDOWS), tp, tp + POOL_HIST), lambda i, j: (0, 0, 0)),
            pl.BlockSpec((len(POOL_WINDOWS), POOL_CH, POOL_CH), lambda i, j: (0, 0, 0)),
            pl.BlockSpec((1, D_POOL), lambda i, j: (0, 0)),
        ],
        out_specs=pl.BlockSpec((1, tp, D_POOL), lambda i, j: (i, j, 0)),
        out_shape=jax.ShapeDtypeStruct((b, l, D_POOL), BF16),
        scratch_shapes=[pltpu.VMEM((POOL_HIST + tp, D_POOL), BF16)],
        compiler_params=pltpu.CompilerParams(
            dimension_semantics=("arbitrary", "arbitrary"), vmem_limit_bytes=VMEM_LIMIT),
        name="pool",
    )(u, band, pw, scale)


def _outproj_kernel(ys_ref, yp_ref, x_ref, ws_ref, wp_ref, g_ref, rwt_ref, rb_ref,
                    x1_ref, h_ref, lt_ref):
    acc = jnp.dot(ys_ref[...], ws_ref[...], preferred_element_type=F32)
    acc = acc + jnp.dot(yp_ref[...], wp_ref[...], preferred_element_type=F32)
    x1 = x_ref[...] + acc
    x1_ref[...] = x1
    h = _rms(x1, g_ref[...])
    _store_packed_rows(h_ref, h)
    lt = lax.dot_general(rwt_ref[...], h.astype(BF16), (((1,), (1,)), ((), ())),
                         preferred_element_type=F32)
    lt_ref[...] = lt + rb_ref[...]


def _outproj(y_ssd, y_pool, x2d, w_s, w_p, g, rwt, rb):
    t = x2d.shape[0]
    tm = min(TM_OUT, t)
    c2 = lambda i: (0, 0)
    return pl.pallas_call(
        _outproj_kernel,
        grid=(t // tm,),
        in_specs=[
            pl.BlockSpec((tm, D_SSD), lambda i: (i, 0)),
            pl.BlockSpec((tm, D_POOL), lambda i: (i, 0)),
            pl.BlockSpec((tm, D_MODEL), lambda i: (i, 0)),
            pl.BlockSpec((D_SSD, D_MODEL), c2),
            pl.BlockSpec((D_POOL, D_MODEL), c2),
            pl.BlockSpec((1, D_MODEL), c2),
            pl.BlockSpec((N_EXPERTS, D_MODEL), c2),
            pl.BlockSpec((N_EXPERTS, 1), c2),
        ],
        out_specs=[
            pl.BlockSpec((tm, D_MODEL), lambda i: (i, 0)),
            pl.BlockSpec((tm * ROW_TILE, LANES), lambda i: (i, 0)),
            pl.BlockSpec((N_EXPERTS, tm), lambda i: (0, i)),
        ],
        out_shape=[
            jax.ShapeDtypeStruct((t, D_MODEL), F32),
            jax.ShapeDtypeStruct((t * ROW_TILE, LANES), U32),
            jax.ShapeDtypeStruct((N_EXPERTS, t), F32),
        ],
        compiler_params=pltpu.CompilerParams(
            dimension_semantics=("arbitrary",), vmem_limit_bytes=VMEM_LIMIT),
        name="outproj",
    )(y_ssd, y_pool, x2d, w_s, w_p, g, rwt, rb)


def _route_kernel(lt_ref, utri_ref, idx_ref, gate_ref, rank_ref, cnt_ref, carry_ref):
    tr = lt_ref.shape[1]

    @pl.when(pl.program_id(0) == 0)
    def _():
        carry_ref[...] = jnp.zeros_like(carry_ref)

    l = lt_ref[...]
    eidx = lax.broadcasted_iota(I32, (N_EXPERTS, tr), 0).astype(F32)
    sels, vals, idxs = [], [], []
    for _ in range(TOP_K):
        m = jnp.max(l, axis=0, keepdims=True)
        ik = jnp.min(jnp.where(l == m, eidx, float(N_EXPERTS)), axis=0, keepdims=True)
        sel = eidx == ik
        l = jnp.where(sel, -jnp.inf, l)
        sels.append(sel)
        vals.append(m)
        idxs.append(ik)
    es = [jnp.exp(v - vals[0]) for v in vals]
    den = es[0] + es[1] + es[2] + es[3]
    multi = jnp.zeros((N_EXPERTS, tr), F32)
    for sel in sels:
        multi = multi + jnp.where(sel, 1.0, 0.0)
    incl = jnp.dot(multi.astype(BF16), utri_ref[...], preferred_element_type=F32)
    excl = incl - multi + carry_ref[:, 0:1]
    for k in range(TOP_K):
        idx_ref[0, k:k + 1, :] = idxs[k].astype(I32)
        gate_ref[0, k:k + 1, :] = es[k] / den
        rk = jnp.sum(jnp.where(sels[k], excl, 0.0), axis=0, keepdims=True)
        rank_ref[0, k:k + 1, :] = rk.astype(I32)
    carry_ref[...] = carry_ref[...] + jnp.sum(multi, axis=1, keepdims=True)
    cnt_ref[...] = carry_ref[...]


def _route(lt, utri):
    t = lt.shape[1]
    tr = utri.shape[0]
    nt = t // tr
    o3 = lambda i: (i, 0, 0)
    return pl.pallas_call(
        _route_kernel,
        grid=(nt,),
        in_specs=[
            pl.BlockSpec((N_EXPERTS, tr), lambda i: (0, i)),
            pl.BlockSpec((tr, tr), lambda i: (0, 0)),
        ],
        out_specs=[
            pl.BlockSpec((1, TOP_K, tr), o3),
            pl.BlockSpec((1, TOP_K, tr), o3),
            pl.BlockSpec((1, TOP_K, tr), o3),
            pl.BlockSpec((N_EXPERTS, LANES), lambda i: (0, 0)),
        ],
        out_shape=[
            jax.ShapeDtypeStruct((nt, TOP_K, tr), I32),
            jax.ShapeDtypeStruct((nt, TOP_K, tr), F32),
            jax.ShapeDtypeStruct((nt, TOP_K, tr), I32),
            jax.ShapeDtypeStruct((N_EXPERTS, LANES), F32),
        ],
        scratch_shapes=[pltpu.VMEM((N_EXPERTS, LANES), F32)],
        compiler_params=pltpu.CompilerParams(
            dimension_semantics=("arbitrary",), vmem_limit_bytes=VMEM_LIMIT),
        name="route",
    )(lt, utri)


def _dest_kernel(pstart_ref, idx_ref, rank_ref, dest_ref):
    idx = idx_ref[...]
    base = jnp.zeros_like(idx)
    for e in range(N_EXPERTS):
        base = jnp.where(idx == e, pstart_ref[e], base)
    dest_ref[...] = base + rank_ref[...]


def _dest(pstart, idx, rank):
    nt, _, tr = idx.shape
    tb = min(DEST_TILES, nt)
    o3 = lambda i, ps: (i, 0, 0)
    return pl.pallas_call(
        _dest_kernel,
        grid_spec=pltpu.PrefetchScalarGridSpec(
            num_scalar_prefetch=1,
            grid=(nt // tb,),
            in_specs=[pl.BlockSpec((tb, TOP_K, tr), o3), pl.BlockSpec((tb, TOP_K, tr), o3)],
            out_specs=pl.BlockSpec((tb, TOP_K, tr), o3),
        ),
        out_shape=jax.ShapeDtypeStruct((nt, TOP_K, tr), I32),
        compiler_params=pltpu.CompilerParams(dimension_semantics=("arbitrary",)),
        name="dest",
    )(pstart, idx, rank)


def _row_tile(r):
    if isinstance(r, int):
        return pl.ds(r * ROW_TILE, ROW_TILE)
    return pl.ds(pl.multiple_of(r * ROW_TILE, ROW_TILE), ROW_TILE)


XS_ROW = ROW_TILE + 1


def _slot_row(r):
    return pl.ds(r * XS_ROW, XS_ROW)


def _dispatch_kernel(cnt_ref, pad_ref, pstart_ref, nu_ref, dest_hbm, h_ref, xs_hbm, dsm, stage, zrow,
                     sem_d, sem, sem_z, *, n_tokens):
    i = pl.program_id(0)
    w = dsm.shape[0] // 2
    tr = w // TOP_K
    zrows = zrow.shape[0]
    nb = xs_hbm.shape[0] // zrows
    slot = lax.rem(i, 2)

    def dest_copy(step, s):
        return pltpu.make_async_copy(dest_hbm.at[step], dsm.at[pl.ds(pl.multiple_of(s * w, w), w)],
                                     sem_d)

    def zero_copy(r):
        return pltpu.make_async_copy(zrow.at[pl.ds(0, XS_ROW), :], xs_hbm.at[_slot_row(r), :], sem)

    def zero_block_copy(blk):
        rows = pl.ds(pl.multiple_of(blk * zrows, zrows), zrows)
        return pltpu.make_async_copy(zrow, xs_hbm.at[rows, :], sem_z)

    @pl.when(i == 0)
    def _():
        dest_copy(0, 0).start()
        zrow[...] = jnp.zeros_like(zrow)

        def zstart(blk, c):
            zero_block_copy(blk).start()
            return c

        def zwait(blk, c):
            zero_block_copy(blk).wait()
            return c

        lax.fori_loop(nu_ref[0], nb, zstart, 0)
        lax.fori_loop(nu_ref[0], nb, zwait, 0)
        for e in range(N_EXPERTS):
            lo, hi, base = cnt_ref[e], pad_ref[e], pstart_ref[e]

            def start(j, c, base=base):
                zero_copy(base + j).start()
                return c

            def wait(j, c, base=base):
                zero_copy(base + j).wait()
                return c

            lax.fori_loop(lo, hi, start, 0)
            lax.fori_loop(lo, hi, wait, 0)

    dest_copy(i, slot).wait()

    @pl.when(i + 1 < pl.num_programs(0))
    def _():
        dest_copy(i + 1, 1 - slot).start()

    tok = i * tr + lax.broadcasted_iota(I32, (tr, LANES), 0)
    for k in range(TOP_K):
        for c in range(ROW_TILE):
            stage[k, pl.ds(c, tr, stride=XS_ROW), :] = h_ref[pl.ds(c, tr, stride=ROW_TILE), :]
        stage[k, pl.ds(ROW_TILE, tr, stride=XS_ROW), :] = (tok + k * n_tokens).astype(U32)

    def row_copy(t, k):
        d = dsm[slot * w + k * tr + t]
        return pltpu.make_async_copy(stage.at[k, _slot_row(t), :], xs_hbm.at[_slot_row(d), :], sem)

    def start(t, c):
        for k in range(TOP_K):
            row_copy(t, k).start(priority=k % 2)
        return c

    def wait(t, c):
        for k in range(TOP_K):
            row_copy(t, k).wait()
        return c

    lax.fori_loop(0, tr, start, 0, unroll=2)
    lax.fori_loop(0, tr, wait, 0, unroll=8)


def _dispatch(counts, padded, pstart, n_used, dest2d, h, n_slots):
    nt, w = dest2d.shape
    tr = w // TOP_K
    return pl.pallas_call(
        functools.partial(_dispatch_kernel, n_tokens=nt * tr),
        grid_spec=pltpu.PrefetchScalarGridSpec(
            num_scalar_prefetch=4,
            grid=(nt,),
            in_specs=[
                pl.BlockSpec(memory_space=pl.ANY),
                pl.BlockSpec((tr * ROW_TILE, LANES), lambda i, *_: (i, 0)),
            ],
            out_specs=pl.BlockSpec(memory_space=pl.ANY),
            scratch_shapes=[
                pltpu.SMEM((2 * w,), I32),
                pltpu.VMEM((TOP_K, tr * XS_ROW, LANES), U32),
                pltpu.VMEM((BM_EXPERT * XS_ROW, LANES), U32),
                pltpu.SemaphoreType.DMA,
                pltpu.SemaphoreType.DMA,
                pltpu.SemaphoreType.DMA,
            ],
        ),
        out_shape=jax.ShapeDtypeStruct((n_slots * XS_ROW, LANES), U32),
        compiler_params=pltpu.CompilerParams(dimension_semantics=("arbitrary",)),
        name="dispatch",
    )(counts, padded, pstart, n_used, dest2d, h)


def _expert_kernel(be_ref, nu_ref, nv_ref, xs_ref, wgu_ref, bgu_ref, wd_ref, bd_ref, y_hbm, xb_s,
                   wgu_s, wd_s, ystage, idv, ids_sm, sem_id, sem_sc):
    i = pl.program_id(0)
    bm = xb_s.shape[0]
    idr = bm // LANES
    nu = nu_ref[0]
    cur = lax.rem(i, 2)
    prev = 1 - cur

    def n_valid(j):
        return jnp.where((j >= 0) & (j < nu), nv_ref[jnp.maximum(j, 0)], 0)

    def id_copy(s):
        rows = pl.ds(s * idr, idr)
        return pltpu.make_async_copy(idv.at[rows, :], ids_sm.at[rows, :], sem_id.at[s])

    def row_send(s, r, rid):
        return pltpu.make_async_copy(ystage.at[s, _row_tile(r), :], y_hbm.at[_row_tile(rid), :],
                                     sem_sc.at[s])

    def wait_rows(s, n):
        one = row_send(s, 0, 0)

        @pl.when(n == bm)
        def _():
            for _ in range(bm):
                one.wait()

        @pl.when(n < bm)
        def _():
            def body(r, c):
                one.wait()
                return c

            lax.fori_loop(0, n, body, 0)

    def send_range(s, r0, r1):
        for r in range(r0, r1):
            row_send(s, r, ids_sm[s * idr + r // LANES, r % LANES]).start(priority=r % 2)

    def send_all(s):
        send_range(s, 0, bm)

    def send_some(s, n):
        group = 64
        for r0 in range(0, bm, group):
            @pl.when(n >= r0 + group)
            def _():
                send_range(s, r0, r0 + group)

        def body(r, c):
            rid = ids_sm[s * idr + jnp.right_shift(r, LANES.bit_length() - 1), jnp.bitwise_and(r, LANES - 1)]
            row_send(s, r, rid).start()
            return c

        lax.fori_loop(jnp.bitwise_and(n, -group), n, body, 0)

    n1 = n_valid(i - 1)
    wait_rows(cur, n_valid(i - 2))

    @pl.when(n1 > 0)
    def _():
        id_copy(prev).wait()

    @pl.when((i == 0) | (be_ref[i] != be_ref[jnp.maximum(i - 1, 0)]))
    def _():
        rc = 128
        for r in range(0, D_MODEL, rc):
            wgu_s[r:r + rc, :] = wgu_ref[0, r:r + rc, :].astype(BF16)
        for r in range(0, D_FF, rc):
            wd_s[r:r + rc, :] = wd_ref[0, r:r + rc, :].astype(BF16)

    def compute():
        for c in range(ROW_TILE):
            lo, hi = _load_packed_chunk(xs_ref, c, bm, XS_ROW)
            xb_s[:, c * LANES:(c + 1) * LANES] = lo.astype(BF16)
            xb_s[:, HALF + c * LANES:HALF + (c + 1) * LANES] = hi.astype(BF16)
        idm = lax.bitcast_convert_type(xs_ref[pl.ds(ROW_TILE, bm, stride=XS_ROW), :], I32).astype(F32)
        diag = (lax.broadcasted_iota(I32, (LANES, LANES), 0)
                == lax.broadcasted_iota(I32, (LANES, LANES), 1))
        for q in range(idr):
            picked = jnp.where(diag, idm[q * LANES:(q + 1) * LANES, :], 0.0)
            idv[pl.ds(cur * idr + q, 1), :] = jnp.sum(picked, axis=0, keepdims=True).astype(I32)
        id_copy(cur).start()
        gu = jnp.dot(xb_s[...], wgu_s[...], preferred_element_type=F32) + bgu_ref[0]
        gt = jnp.minimum(gu[:, :D_FF], SWIGLU_LIMIT)
        up = jnp.clip(gu[:, D_FF:], -SWIGLU_LIMIT, SWIGLU_LIMIT)
        hg = 0.5 * gt
        act = (up + 1.0) * (hg + hg * jnp.tanh(SWIGLU_ALPHA * hg))
        y = jnp.dot(act.astype(BF16), wd_s[...], preferred_element_type=F32) + bd_ref[0]
        _store_packed_rows(ystage.at[cur], y)

    active = i < nu
    prev_full = n1 == bm

    @pl.when(active & prev_full)
    def _():
        compute()
        send_all(prev)

    @pl.when(active & jnp.logical_not(prev_full))
    def _():
        send_some(prev, n1)
        compute()

    @pl.when(jnp.logical_not(active))
    def _():
        send_some(prev, n1)


def _experts(block_e, n_used, n_valid, xs, wgu, bgu, wd, bd, n_tokens):
    bm = BM_EXPERT
    steps = block_e.shape[0]
    row = lambda i, be, nu, nv: (jnp.minimum(i, nu[0] - 1), 0)
    wsel = lambda i, be, nu, nv: (be[i], 0, 0)
    return pl.pallas_call(
        _expert_kernel,
        grid_spec=pltpu.PrefetchScalarGridSpec(
            num_scalar_prefetch=3,
            grid=(steps,),
            in_specs=[
                pl.BlockSpec((bm * XS_ROW, LANES), row),
                pl.BlockSpec((1, D_MODEL, 2 * D_FF), wsel),
                pl.BlockSpec((1, 1, 2 * D_FF), wsel),
                pl.BlockSpec((1, D_FF, D_MODEL), wsel),
                pl.BlockSpec((1, 1, D_MODEL), wsel),
            ],
            out_specs=pl.BlockSpec(memory_space=pl.ANY),
            scratch_shapes=[
                pltpu.VMEM((bm, D_MODEL), BF16),
                pltpu.VMEM((D_MODEL, 2 * D_FF), BF16),
                pltpu.VMEM((D_FF, D_MODEL), BF16),
                pltpu.VMEM((2, bm * ROW_TILE, LANES), U32),
                pltpu.VMEM((2 * (bm // LANES), LANES), I32),
                pltpu.SMEM((2 * (bm // LANES), LANES), I32),
                pltpu.SemaphoreType.DMA((2,)),
                pltpu.SemaphoreType.DMA((2,)),
            ],
        ),
        out_shape=jax.ShapeDtypeStruct((TOP_K * n_tokens * ROW_TILE, LANES), U32),
        compiler_params=pltpu.CompilerParams(
            dimension_semantics=("arbitrary",), vmem_limit_bytes=VMEM_LIMIT),
        name="experts",
    )(block_e, n_used, n_valid, xs, wgu, bgu, wd, bd)


def _combine_kernel(y0_ref, y1_ref, y2_ref, y3_ref, x1_ref, gate_ref, p_ref, gng_ref, wg_ref,
                    wp_ref, png_ref, fng_ref, o_ref):
    tr = x1_ref.shape[0]
    y_refs = (y0_ref, y1_ref, y2_ref, y3_ref)
    lo_chunks, hi_chunks = [], []
    for c in range(ROW_TILE):
        acc_lo = x1_ref[:, c * LANES:(c + 1) * LANES]
        acc_hi = x1_ref[:, HALF + c * LANES:HALF + (c + 1) * LANES]
        for k in range(TOP_K):
            lo, hi = _load_packed_chunk(y_refs[k], c, tr)
            g = gate_ref[:, k:k + 1]
            acc_lo = acc_lo + g * lo
            acc_hi = acc_hi + g * hi
        lo_chunks.append(acc_lo)
        hi_chunks.append(acc_hi)
    x2 = jnp.concatenate(lo_chunks + hi_chunks, axis=1)
    gate = jnp.dot(_rms(x2, gng_ref[...]).astype(BF16), wg_ref[...], preferred_element_type=F32)
    gate = 1.0 / (1.0 + jnp.exp(-gate))
    e = jnp.dot(p_ref[...].astype(BF16), wp_ref[...], preferred_element_type=F32)
    x3 = x2 + _rms(e, png_ref[...]) * gate
    o_ref[...] = _rms(x3, fng_ref[...])


def _combine(y, x1, gate_t, p2d, gng, wg, wp, png, fng):
    t = x1.shape[0]
    tr = min(TR_ROUTE, t)
    nt = t // tr
    c2 = lambda i: (0, 0)
    y_specs = [pl.BlockSpec((tr * ROW_TILE, LANES), functools.partial(lambda i, k: (k * nt + i, 0), k=k))
               for k in range(TOP_K)]
    return pl.pallas_call(
        _combine_kernel,
        grid=(nt,),
        in_specs=y_specs + [
            pl.BlockSpec((tr, D_MODEL), lambda i: (i, 0)),
            pl.BlockSpec((tr, TOP_K), lambda i: (i, 0)),
            pl.BlockSpec((tr, PLE_DIM), lambda i: (i, 0)),
            pl.BlockSpec((1, D_MODEL), c2),
            pl.BlockSpec((D_MODEL, D_MODEL), c2),
            pl.BlockSpec((PLE_DIM, D_MODEL), c2),
            pl.BlockSpec((1, D_MODEL), c2),
            pl.BlockSpec((1, D_MODEL), c2),
        ],
        out_specs=pl.BlockSpec((tr, D_MODEL), lambda i: (i, 0)),
        out_shape=jax.ShapeDtypeStruct((t, D_MODEL), F32),
        compiler_params=pltpu.CompilerParams(
            dimension_semantics=("arbitrary",), vmem_limit_bytes=VMEM_LIMIT),
        name="combine",
    )(y, y, y, y, x1, gate_t, p2d, gng, wg, wp, png, fng)


def _pad_heads(v):
    return jnp.pad(v.astype(F32), (0, HEAD_PAD - SSD_HEADS)).reshape(1, HEAD_PAD)


def _layer(x, p, mix_norm_g, w_in, conv_w, conv_b, dt_bias, a_log, d_skip, ssd_norm_g, pool_w,
           pool_scale, w_out, ffn_norm_g, router_w, router_b, w_gate_up, b_gate_up, w_down,
           b_down, ple_gate_norm_g, w_ple_gate, w_ple_proj, ple_norm_g, final_g):
    b, l, _ = x.shape
    t = b * l
    x2d = x.reshape(t, D_MODEL)
    row = lambda v: v.reshape(1, -1).astype(F32)

    c_dt = D_SSD + D_CONV
    c_pool = c_dt + SSD_HEADS
    w_cat = jnp.concatenate(
        [w_in[:, :c_dt], w_in[:, c_pool:], w_in[:, c_dt:c_pool],
         jnp.zeros((D_MODEL, HEAD_PAD - SSD_HEADS), w_in.dtype)], axis=1).astype(BF16)
    ii = jnp.arange(CHUNK)
    tri = (ii[None, :] <= ii[:, None]).astype(BF16)
    tp = min(TP_POOL, l)
    rr = jnp.arange(tp)[:, None] + POOL_HIST
    jj = jnp.arange(tp + POOL_HIST)[None, :]
    band = jnp.stack([((jj <= rr) & (jj > rr - w)) for w in POOL_WINDOWS]).astype(BF16)
    tr = min(TR_ROUTE, t)
    ri = jnp.arange(tr)
    utri = (ri[:, None] <= ri[None, :]).astype(BF16)

    head_of_col = jnp.arange(D_SSD) // SSD_HEAD_DIM
    expand = (jnp.arange(HEAD_PAD)[:, None] == head_of_col[None, :]).astype(BF16)

    z, xbc, pool_in, dt = _inproj(x2d, row(mix_norm_g), w_cat)
    y_ssd = _ssd(xbc.reshape(b, l, D_CONV), z.reshape(b, l, D_SSD), dt.reshape(b, l, HEAD_PAD),
                 conv_w.astype(F32), row(conv_b), _pad_heads(dt_bias), _pad_heads(a_log),
                 row(jnp.repeat(d_skip, SSD_HEAD_DIM)), row(ssd_norm_g), tri, expand)
    y_pool = _pool(pool_in.reshape(b, l, D_POOL), band, pool_w.astype(BF16), row(pool_scale))
    x1, h, lt = _outproj(y_ssd.reshape(t, D_SSD), y_pool.reshape(t, D_POOL), x2d,
                         w_out[:D_SSD].astype(BF16), w_out[D_SSD:].astype(BF16), row(ffn_norm_g),
                         router_w.T.astype(BF16), router_b.reshape(N_EXPERTS, 1).astype(F32))

    idx, gate, rank, cnt = _route(lt, utri)
    counts = cnt[:, 0].astype(I32)
    bm = BM_EXPERT
    padded = ((counts + bm - 1) // bm) * bm
    pend = jnp.cumsum(padded)
    pstart = (pend - padded).astype(I32)
    nb = (t * TOP_K) // bm + N_EXPERTS
    n_used = (pend[-1] // bm).astype(I32)
    blk = jnp.arange(nb + 1, dtype=I32)
    block_e = jnp.sum((pend[None, :] <= (blk * bm)[:, None]).astype(I32), axis=1)
    block_e = jnp.minimum(block_e, N_EXPERTS - 1)
    last_e = jnp.sum((pend <= (n_used - 1) * bm).astype(I32))
    block_e = jnp.where(blk < n_used, block_e, jnp.minimum(last_e, N_EXPERTS - 1)).astype(I32)
    n_valid = jnp.clip(counts[block_e] - (blk * bm - pstart[block_e]), 0, bm)
    n_valid = jnp.where(blk < n_used, n_valid, 0).astype(I32)

    dest = _dest(pstart, idx, rank)
    nt = t // tr
    dest2d = dest.reshape(nt, TOP_K * tr)
    xs = _dispatch(counts, padded.astype(I32), pstart, n_used.reshape(1), dest2d, h, nb * bm)
    y = _experts(block_e, n_used.reshape(1), n_valid, xs, w_gate_up.astype(F32),
                 b_gate_up.reshape(N_EXPERTS, 1, 2 * D_FF).astype(F32), w_down.astype(F32),
                 b_down.reshape(N_EXPERTS, 1, D_MODEL).astype(F32), t)
    gate_t = gate.transpose(0, 2, 1).reshape(t, TOP_K)
    out = _combine(y, x1, gate_t, p.reshape(t, PLE_DIM), row(ple_gate_norm_g),
                   w_ple_gate.astype(BF16), w_ple_proj.astype(BF16), row(ple_norm_g), row(final_g))
    return out.reshape(b, l, D_MODEL)


def kernel(x, p, mix_norm_g, w_in, conv_w, conv_b, dt_bias, a_log, d_skip, ssd_norm_g, pool_w, pool_scale, w_out, ffn_norm_g, router_w, router_b, w_gate_up, b_gate_up, w_down, b_down, ple_gate_norm_g, w_ple_gate, w_ple_proj, ple_norm_g, final_norm_g):
    assert x.shape[-1] == D_MODEL and mix_norm_g.shape[0] == 1
    i = 0
    return _layer(x, p[i], mix_norm_g[i], w_in[i], conv_w[i], conv_b[i], dt_bias[i], a_log[i],
                  d_skip[i], ssd_norm_g[i], pool_w[i], pool_scale[i], w_out[i], ffn_norm_g[i],
                  router_w[i], router_b[i], w_gate_up[i], b_gate_up[i], w_down[i], b_down[i],
                  ple_gate_norm_g[i], w_ple_gate[i], w_ple_proj[i], ple_norm_g[i], final_norm_g)
```

```python
import functools

import jax
import jax.numpy as jnp
from jax import lax
from jax.experimental import pallas as pl
from jax.experimental.pallas import tpu as pltpu

F32 = jnp.float32
BF16 = jnp.bfloat16
I32 = jnp.int32

D_MODEL = 1024
SSD_HEADS = 32
SSD_HEAD_DIM = 64
D_SSD = SSD_HEADS * SSD_HEAD_DIM
SSD_GROUPS = 8
HEADS_PER_GROUP = SSD_HEADS // SSD_GROUPS
GROUP_W = HEADS_PER_GROUP * SSD_HEAD_DIM
D_STATE = 128
CONV_WIDTH = 4
CHUNK = 128
D_BC = SSD_GROUPS * D_STATE
D_CONV = D_SSD + 2 * D_BC
POOL_WINDOWS = (2, 4, 8, 16)
POOL_CH = 256
D_POOL = len(POOL_WINDOWS) * POOL_CH
POOL_HIST = 128
N_EXPERTS = 32
TOP_K = 4
D_FF = 1024
SWIGLU_LIMIT = 7.0
SWIGLU_ALPHA = 1.702
PLE_DIM = 256
EPS = 1e-6

LANES = 128
HEAD_PAD = LANES

TM_IN = 512
TM_OUT = 512
TR_ROUTE = 256
DEST_TILES = 16
BM_EXPERT = 512
FF_CHUNK = 512
VMEM_LIMIT = 56 * 1024 * 1024


def _rms(x, g):
    return x * lax.rsqrt(jnp.mean(x * x, axis=-1, keepdims=True) + EPS) * g


def _silu(x):
    h = 0.5 * x
    return h + h * jnp.tanh(h)


HALF = D_MODEL // 2
ROW_TILE = HALF // LANES
U32 = jnp.uint32


def _store_packed_rows(ref, v):
    m = v.shape[0]
    u = lax.bitcast_convert_type(v, U32)
    r = u + jnp.uint32(0x7FFF) + ((u >> 16) & jnp.uint32(1))
    w = (r[:, :HALF] >> 16) | (r[:, HALF:] & jnp.uint32(0xFFFF0000))
    for c in range(ROW_TILE):
        ref[pl.ds(c, m, stride=ROW_TILE), :] = w[:, c * LANES:(c + 1) * LANES]


def _load_packed_chunk(ref, c, m):
    w = ref[pl.ds(c, m, stride=ROW_TILE), :]
    lo = lax.bitcast_convert_type(w << 16, F32)
    hi = lax.bitcast_convert_type(w & jnp.uint32(0xFFFF0000), F32)
    return lo, hi


def _split3(v):
    hi = v.astype(BF16)
    r1 = v - hi.astype(F32)
    mid = r1.astype(BF16)
    lo = (r1 - mid.astype(F32)).astype(BF16)
    return hi, mid, lo


W_IN_COLS = D_SSD + D_CONV + D_POOL + HEAD_PAD


def _inproj_kernel(x_ref, g_ref, w_ref, z_ref, xbc_ref, pool_ref, dt_ref):
    hb = _rms(x_ref[...], g_ref[...]).astype(BF16)

    def mm(c0, n):
        return jnp.dot(hb, w_ref[:, c0:c0 + n], preferred_element_type=F32)

    cw = 512
    for c in range(0, D_SSD, cw):
        z_ref[:, c:c + cw] = mm(c, cw).astype(BF16)
    for c in range(0, D_CONV, cw):
        xbc_ref[:, c:c + cw] = mm(D_SSD + c, cw).astype(BF16)
    for c in range(0, D_POOL, cw):
        pool_ref[:, c:c + cw] = mm(D_SSD + D_CONV + c, cw).astype(BF16)
    dt_ref[...] = mm(D_SSD + D_CONV + D_POOL, HEAD_PAD)


def _inproj(x2d, g, w_cat):
    t = x2d.shape[0]
    tm = min(TM_IN, t)
    return pl.pallas_call(
        _inproj_kernel,
        grid=(t // tm,),
        in_specs=[
            pl.BlockSpec((tm, D_MODEL), lambda i: (i, 0)),
            pl.BlockSpec((1, D_MODEL), lambda i: (0, 0)),
            pl.BlockSpec((D_MODEL, W_IN_COLS), lambda i: (0, 0), pipeline_mode=pl.Buffered(1)),
        ],
        out_specs=[
            pl.BlockSpec((tm, D_SSD), lambda i: (i, 0)),
            pl.BlockSpec((tm, D_CONV), lambda i: (i, 0)),
            pl.BlockSpec((tm, D_POOL), lambda i: (i, 0)),
            pl.BlockSpec((tm, HEAD_PAD), lambda i: (i, 0)),
        ],
        out_shape=[
            jax.ShapeDtypeStruct((t, D_SSD), BF16),
            jax.ShapeDtypeStruct((t, D_CONV), BF16),
            jax.ShapeDtypeStruct((t, D_POOL), BF16),
            jax.ShapeDtypeStruct((t, HEAD_PAD), F32),
        ],
        compiler_params=pltpu.CompilerParams(
            dimension_semantics=("arbitrary",), vmem_limit_bytes=VMEM_LIMIT),
        name="inproj",
    )(x2d, g, w_cat)


def _ssd_kernel(xbc_ref, z_ref, dt_ref, convw_ref, convb_ref, dtb_ref, alog_ref, dskip_ref,
                ng_ref, tri_ref, expand_ref, u_ref, band_ref, pw_ref, pscale_ref, y_ref, yp_ref,
                state_ref, hist_ref, xs_s, b_s, c_s, phist_ref):
    q = CHUNK
    li = pl.program_id(1)

    @pl.when(li == 0)
    def _():
        state_ref[...] = jnp.zeros_like(state_ref)
        hist_ref[...] = jnp.zeros_like(hist_ref)
        phist_ref[0:POOL_HIST, :] = jnp.zeros((POOL_HIST, D_POOL), BF16)

    phist_ref[POOL_HIST:POOL_HIST + q, :] = u_ref[0]
    pos = li * q + lax.broadcasted_iota(I32, (q, POOL_CH), 0)
    for gi, w in enumerate(POOL_WINDOWS):
        ps = slice(gi * POOL_CH, (gi + 1) * POOL_CH)
        win_sum = jnp.dot(band_ref[gi], phist_ref[:, ps], preferred_element_type=F32)
        cnt = jnp.minimum(pos + 1, w).astype(F32)
        pre = (win_sum / cnt - u_ref[0, :, ps].astype(F32)).astype(BF16)
        yp = jnp.dot(pre, pw_ref[gi], preferred_element_type=F32) * pscale_ref[:, ps]
        yp_ref[0, :, ps] = yp.astype(BF16)
    phist_ref[0:POOL_HIST, :] = phist_ref[q:q + POOL_HIST, :]

    cw = GROUP_W
    hr = hist_ref.shape[0]
    for ci in range(D_CONV // cw):
        cs = slice(ci * cw, (ci + 1) * cw)
        cur = xbc_ref[0, :, cs].astype(F32)
        xe = jnp.concatenate([hist_ref[:, cs], cur], axis=0)
        hist_ref[:, cs] = cur[q - hr:q, :]
        s1 = pltpu.roll(xe, 1, 0)
        u = convw_ref[3:4, cs] * xe + convw_ref[2:3, cs] * s1
        w = convw_ref[1:2, cs] * xe + convw_ref[0:1, cs] * s1
        acc = convb_ref[:, cs] + u + pltpu.roll(w, 2, 0)
        v = _silu(acc[hr:, :])
        if ci < D_SSD // cw:
            xs_s[ci] = v
        elif ci < (D_SSD + D_BC) // cw:
            j = ci - D_SSD // cw
            b_s[2 * j] = v[:, :D_STATE]
            b_s[2 * j + 1] = v[:, D_STATE:]
        else:
            j = ci - (D_SSD + D_BC) // cw
            c_s[2 * j] = v[:, :D_STATE].astype(BF16)
            c_s[2 * j + 1] = v[:, D_STATE:].astype(BF16)

    xdt = dt_ref[0] + dtb_ref[...]
    dt = jnp.maximum(xdt, 0.0) + jnp.log1p(jnp.exp(-jnp.abs(xdt)))
    a = dt * (-jnp.exp(alog_ref[...]))
    tri = tri_ref[...]
    a_hi, a_mid, a_lo = _split3(a)
    acs = (jnp.dot(tri, a_hi, preferred_element_type=F32)
           + jnp.dot(tri, a_mid, preferred_element_type=F32)
           + jnp.dot(tri, a_lo, preferred_element_type=F32))
    acs_t = acs.T
    dt_t = dt.T
    eacs = jnp.exp(acs)
    w_t = jnp.exp(acs_t[:, q - 1:q] - acs_t) * dt_t
    e_hi = eacs.astype(BF16)
    e_lo = (eacs - e_hi.astype(F32)).astype(BF16)

    row = lax.broadcasted_iota(I32, (q, q), 0)
    col = lax.broadcasted_iota(I32, (q, q), 1)
    causal = col <= row
    head_shift = SSD_HEAD_DIM.bit_length() - 1
    lane_head = jnp.right_shift(lax.broadcasted_iota(I32, (q, GROUP_W), 1), head_shift)
    head_mask = [jnp.where(lane_head == r, 1.0, 0.0).astype(BF16) for r in range(HEADS_PER_GROUP)]

    for g in range(SSD_GROUPS):
        gs = slice(g * GROUP_W, (g + 1) * GROUP_W)
        xs_g = xs_s[g]
        b_g = b_s[g]
        c_g = c_s[g]
        cb = lax.dot_general(c_g, b_g.astype(BF16), (((1,), (1,)), ((), ())),
                             preferred_element_type=F32)
        bt_g = b_g.T
        s_old = state_ref[g]
        y_off = jnp.dot(c_g, s_old.astype(BF16), preferred_element_type=F32)
        esc = (jnp.dot(e_hi, expand_ref[:, gs], preferred_element_type=F32)
               + jnp.dot(e_lo, expand_ref[:, gs], preferred_element_type=F32))
        xs_b = xs_g.astype(BF16)
        m_cat, bts_cat = [], []
        for r in range(HEADS_PER_GROUP):
            h = g * HEADS_PER_GROUP + r
            seg = acs[:, h:h + 1] - acs_t[h:h + 1, :]
            dec = jnp.where(causal, jnp.exp(seg), 0.0)
            m_cat.append((cb * dec * dt_t[h:h + 1, :]).astype(BF16))
            bts_cat.append((bt_g * w_t[h:h + 1, :]).astype(BF16))
        xs_bd = jnp.concatenate([xs_b * head_mask[r] for r in range(HEADS_PER_GROUP)], axis=0)
        y_diag = jnp.dot(jnp.concatenate(m_cat, axis=1), xs_bd, preferred_element_type=F32)
        ds = jnp.dot(jnp.concatenate(bts_cat, axis=1), xs_bd, preferred_element_type=F32)
        state_ref[g] = s_old * esc[q - 1:q, :] + ds
        y = y_diag + y_off * esc + dskip_ref[:, gs] * xs_g
        y = y * _silu(z_ref[0, :, gs].astype(F32))
        y_ref[0, :, gs] = _rms(y, ng_ref[:, gs]).astype(BF16)


def _ssd(xbc, z, dt, conv_w, conv_b, dtb, alog, dskip, ng, tri, expand, u, band, pw, pscale):
    b, l, _ = xbc.shape
    nc = l // CHUNK
    cmap = lambda i, c: (0, 0)
    cmap3 = lambda i, c: (0, 0, 0)
    blk = lambda i, c: (i, c, 0)
    return pl.pallas_call(
        _ssd_kernel,
        grid=(b, nc),
        in_specs=[
            pl.BlockSpec((1, CHUNK, D_CONV), lambda i, c: (i, c, 0)),
            pl.BlockSpec((1, CHUNK, D_SSD), lambda i, c: (i, c, 0)),
            pl.BlockSpec((1, CHUNK, HEAD_PAD), lambda i, c: (i, c, 0)),
            pl.BlockSpec((CONV_WIDTH, D_CONV), cmap),
            pl.BlockSpec((1, D_CONV), cmap),
            pl.BlockSpec((1, HEAD_PAD), cmap),
            pl.BlockSpec((1, HEAD_PAD), cmap),
            pl.BlockSpec((1, D_SSD), cmap),
            pl.BlockSpec((1, D_SSD), cmap),
            pl.BlockSpec((CHUNK, CHUNK), cmap),
            pl.BlockSpec((HEAD_PAD, D_SSD), cmap),
            pl.BlockSpec((1, CHUNK, D_POOL), blk),
            pl.BlockSpec((len(POOL_WINDOWS), CHUNK, CHUNK + POOL_HIST), cmap3),
            pl.BlockSpec((len(POOL_WINDOWS), POOL_CH, POOL_CH), cmap3),
            pl.BlockSpec((1, D_POOL), cmap),
        ],
        out_specs=[pl.BlockSpec((1, CHUNK, D_SSD), blk), pl.BlockSpec((1, CHUNK, D_POOL), blk)],
        out_shape=[jax.ShapeDtypeStruct((b, l, D_SSD), BF16),
                   jax.ShapeDtypeStruct((b, l, D_POOL), BF16)],
        scratch_shapes=[
            pltpu.VMEM((SSD_GROUPS, D_STATE, GROUP_W), F32),
            pltpu.VMEM((8, D_CONV), F32),
            pltpu.VMEM((SSD_GROUPS, CHUNK, GROUP_W), F32),
            pltpu.VMEM((SSD_GROUPS, CHUNK, D_STATE), F32),
            pltpu.VMEM((SSD_GROUPS, CHUNK, D_STATE), BF16),
            pltpu.VMEM((POOL_HIST + CHUNK, D_POOL), BF16),
        ],
        compiler_params=pltpu.CompilerParams(
            dimension_semantics=("arbitrary", "arbitrary"), vmem_limit_bytes=VMEM_LIMIT),
        name="ssd",
    )(xbc, z, dt, conv_w, conv_b, dtb, alog, dskip, ng, tri, expand, u, band, pw, pscale)


def _outproj_kernel(ys_ref, yp_ref, x_ref, ws_ref, wp_ref, g_ref, rwt_ref, rb_ref,
                    x1_ref, h_ref, lt_ref):
    acc = jnp.dot(ys_ref[...], ws_ref[...], preferred_element_type=F32)
    acc = acc + jnp.dot(yp_ref[...], wp_ref[...], preferred_element_type=F32)
    x1 = x_ref[...] + acc
    x1_ref[...] = x1
    h = _rms(x1, g_ref[...])
    _store_packed_rows(h_ref, h)
    lt = lax.dot_general(rwt_ref[...], h.astype(BF16), (((1,), (1,)), ((), ())),
                         preferred_element_type=F32)
    lt_ref[...] = lt + rb_ref[...]


def _outproj(y_ssd, y_pool, x2d, w_s, w_p, g, rwt, rb):
    t = x2d.shape[0]
    tm = min(TM_OUT, t)
    c2 = lambda i: (0, 0)
    return pl.pallas_call(
        _outproj_kernel,
        grid=(t // tm,),
        in_specs=[
            pl.BlockSpec((tm, D_SSD), lambda i: (i, 0)),
            pl.BlockSpec((tm, D_POOL), lambda i: (i, 0)),
            pl.BlockSpec((tm, D_MODEL), lambda i: (i, 0)),
            pl.BlockSpec((D_SSD, D_MODEL), c2),
            pl.BlockSpec((D_POOL, D_MODEL), c2),
            pl.BlockSpec((1, D_MODEL), c2),
            pl.BlockSpec((N_EXPERTS, D_MODEL), c2),
            pl.BlockSpec((N_EXPERTS, 1), c2),
        ],
        out_specs=[
            pl.BlockSpec((tm, D_MODEL), lambda i: (i, 0)),
            pl.BlockSpec((tm * ROW_TILE, LANES), lambda i: (i, 0)),
            pl.BlockSpec((N_EXPERTS, tm), lambda i: (0, i)),
        ],
        out_shape=[
            jax.ShapeDtypeStruct((t, D_MODEL), F32),
            jax.ShapeDtypeStruct((t * ROW_TILE, LANES), U32),
            jax.ShapeDtypeStruct((N_EXPERTS, t), F32),
        ],
        compiler_params=pltpu.CompilerParams(
            dimension_semantics=("arbitrary",), vmem_limit_bytes=VMEM_LIMIT),
        name="outproj",
    )(y_ssd, y_pool, x2d, w_s, w_p, g, rwt, rb)


def _route_kernel(lt_ref, utri_ref, idx_ref, gate_ref, rank_ref, cnt_ref, carry_ref):
    tr = lt_ref.shape[1]

    @pl.when(pl.program_id(0) == 0)
    def _():
        carry_ref[...] = jnp.zeros_like(carry_ref)

    l = lt_ref[...]
    eidx = lax.broadcasted_iota(I32, (N_EXPERTS, tr), 0).astype(F32)
    sels, vals, idxs = [], [], []
    for _ in range(TOP_K):
        m = jnp.max(l, axis=0, keepdims=True)
        ik = jnp.min(jnp.where(l == m, eidx, float(N_EXPERTS)), axis=0, keepdims=True)
        sel = eidx == ik
        l = jnp.where(sel, -jnp.inf, l)
        sels.append(sel)
        vals.append(m)
        idxs.append(ik)
    es = [jnp.exp(v - vals[0]) for v in vals]
    den = es[0] + es[1] + es[2] + es[3]
    multi = jnp.zeros((N_EXPERTS, tr), F32)
    for sel in sels:
        multi = multi + jnp.where(sel, 1.0, 0.0)
    incl = jnp.dot(multi.astype(BF16), utri_ref[...], preferred_element_type=F32)
    excl = incl - multi + carry_ref[:, 0:1]
    for k in range(TOP_K):
        idx_ref[0, k:k + 1, :] = idxs[k].astype(I32)
        gate_ref[0, k:k + 1, :] = es[k] / den
        rk = jnp.sum(jnp.where(sels[k], excl, 0.0), axis=0, keepdims=True)
        rank_ref[0, k:k + 1, :] = rk.astype(I32)
    carry_ref[...] = carry_ref[...] + jnp.sum(multi, axis=1, keepdims=True)
    cnt_ref[...] = carry_ref[...]


def _route(lt, utri):
    t = lt.shape[1]
    tr = utri.shape[0]
    nt = t // tr
    o3 = lambda i: (i, 0, 0)
    return pl.pallas_call(
        _route_kernel,
        grid=(nt,),
        in_specs=[
            pl.BlockSpec((N_EXPERTS, tr), lambda i: (0, i)),
            pl.BlockSpec((tr, tr), lambda i: (0, 0)),
        ],
        out_specs=[
            pl.BlockSpec((1, TOP_K, tr), o3),
            pl.BlockSpec((1, TOP_K, tr), o3),
            pl.BlockSpec((1, TOP_K, tr), o3),
            pl.BlockSpec((N_EXPERTS, LANES), lambda i: (0, 0)),
        ],
        out_shape=[
            jax.ShapeDtypeStruct((nt, TOP_K, tr), I32),
            jax.ShapeDtypeStruct((nt, TOP_K, tr), F32),
            jax.ShapeDtypeStruct((nt, TOP_K, tr), I32),
            jax.ShapeDtypeStruct((N_EXPERTS, LANES), F32),
        ],
        scratch_shapes=[pltpu.VMEM((N_EXPERTS, LANES), F32)],
        compiler_params=pltpu.CompilerParams(
            dimension_semantics=("arbitrary",), vmem_limit_bytes=VMEM_LIMIT),
        name="route",
    )(lt, utri)


def _dest_kernel(pstart_ref, idx_ref, rank_ref, dest_ref):
    idx = idx_ref[...]
    base = jnp.zeros_like(idx)
    for e in range(N_EXPERTS):
        base = jnp.where(idx == e, pstart_ref[e], base)
    dest_ref[...] = base + rank_ref[...]


def _dest(pstart, idx, rank):
    nt, _, tr = idx.shape
    tb = min(DEST_TILES, nt)
    o3 = lambda i, ps: (i, 0, 0)
    return pl.pallas_call(
        _dest_kernel,
        grid_spec=pltpu.PrefetchScalarGridSpec(
            num_scalar_prefetch=1,
            grid=(nt // tb,),
            in_specs=[pl.BlockSpec((tb, TOP_K, tr), o3), pl.BlockSpec((tb, TOP_K, tr), o3)],
            out_specs=pl.BlockSpec((tb, TOP_K, tr), o3),
        ),
        out_shape=jax.ShapeDtypeStruct((nt, TOP_K, tr), I32),
        compiler_params=pltpu.CompilerParams(dimension_semantics=("arbitrary",)),
        name="dest",
    )(pstart, idx, rank)


def _row_tile(r):
    return pl.ds(pl.multiple_of(r * ROW_TILE, ROW_TILE), ROW_TILE)


def _dispatch_kernel(cnt_ref, pad_ref, pstart_ref, nu_ref, dest_hbm, h_ref, xs_hbm, dsm, zrow, sem_d,
                     sem, sem_z):
    i = pl.program_id(0)
    w = dsm.shape[0] // 2
    tr = w // TOP_K
    zrows = zrow.shape[0]
    nb = xs_hbm.shape[0] // zrows
    slot = lax.rem(i, 2)

    def dest_copy(step, s):
        return pltpu.make_async_copy(dest_hbm.at[step], dsm.at[pl.ds(pl.multiple_of(s * w, w), w)],
                                     sem_d)

    def zero_copy(r):
        return pltpu.make_async_copy(zrow.at[pl.ds(0, ROW_TILE), :], xs_hbm.at[_row_tile(r), :], sem)

    def zero_block_copy(blk):
        rows = pl.ds(pl.multiple_of(blk * zrows, zrows), zrows)
        return pltpu.make_async_copy(zrow, xs_hbm.at[rows, :], sem_z)

    @pl.when(i == 0)
    def _():
        dest_copy(0, 0).start()
        zrow[...] = jnp.zeros_like(zrow)

        def zstart(blk, c):
            zero_block_copy(blk).start()
            return c

        def zwait(blk, c):
            zero_block_copy(blk).wait()
            return c

        lax.fori_loop(nu_ref[0], nb, zstart, 0)
        lax.fori_loop(nu_ref[0], nb, zwait, 0)
        for e in range(N_EXPERTS):
            lo, hi, base = cnt_ref[e], pad_ref[e], pstart_ref[e]

            def start(j, c, base=base):
                zero_copy(base + j).start()
                return c

            def wait(j, c, base=base):
                zero_copy(base + j).wait()
                return c

            lax.fori_loop(lo, hi, start, 0)
            lax.fori_loop(lo, hi, wait, 0)

    dest_copy(i, slot).wait()

    @pl.when(i + 1 < pl.num_programs(0))
    def _():
        dest_copy(i + 1, 1 - slot).start()

    def row_copy(t, k):
        d = dsm[slot * w + k * tr + t]
        return pltpu.make_async_copy(h_ref.at[_row_tile(t), :], xs_hbm.at[_row_tile(d), :], sem)

    def start(t, c):
        for k in range(TOP_K):
            row_copy(t, k).start(priority=k % 2)
        return c

    def wait(t, c):
        for k in range(TOP_K):
            row_copy(t, k).wait()
        return c

    lax.fori_loop(0, tr, start, 0, unroll=2)
    lax.fori_loop(0, tr, wait, 0, unroll=8)


def _dispatch(counts, padded, pstart, n_used, dest2d, h, n_slots):
    nt, w = dest2d.shape
    tr = w // TOP_K
    return pl.pallas_call(
        _dispatch_kernel,
        grid_spec=pltpu.PrefetchScalarGridSpec(
            num_scalar_prefetch=4,
            grid=(nt,),
            in_specs=[
                pl.BlockSpec(memory_space=pl.ANY),
                pl.BlockSpec((tr * ROW_TILE, LANES), lambda i, *_: (i, 0)),
            ],
            out_specs=pl.BlockSpec(memory_space=pl.ANY),
            scratch_shapes=[
                pltpu.SMEM((2 * w,), I32),
                pltpu.VMEM((BM_EXPERT * ROW_TILE, LANES), U32),
                pltpu.SemaphoreType.DMA,
                pltpu.SemaphoreType.DMA,
                pltpu.SemaphoreType.DMA,
            ],
        ),
        out_shape=jax.ShapeDtypeStruct((n_slots * ROW_TILE, LANES), U32),
        compiler_params=pltpu.CompilerParams(dimension_semantics=("arbitrary",)),
        name="dispatch",
    )(counts, padded, pstart, n_used, dest2d, h)


def _expert_kernel(be_ref, nu_ref, xs_ref, wgu_ref, bgu_ref, wd_ref, bd_ref, ys_ref, xb_s, wgu_s,
                   wd_s):
    i = pl.program_id(0)
    bm = xb_s.shape[0]

    @pl.when((i == 0) | (be_ref[i] != be_ref[jnp.maximum(i - 1, 0)]))
    def _():
        rc = 128
        for r in range(0, D_MODEL, rc):
            wgu_s[r:r + rc, :] = wgu_ref[0, r:r + rc, :].astype(BF16)
        for r in range(0, D_FF, rc):
            wd_s[r:r + rc, :] = wd_ref[0, r:r + rc, :].astype(BF16)

    @pl.when(i < nu_ref[0])
    def _():
        for c in range(ROW_TILE):
            lo, hi = _load_packed_chunk(xs_ref, c, bm)
            xb_s[:, c * LANES:(c + 1) * LANES] = lo.astype(BF16)
            xb_s[:, HALF + c * LANES:HALF + (c + 1) * LANES] = hi.astype(BF16)
        xb = xb_s[...]
        y = bd_ref[0]
        for f in range(0, D_FF, FF_CHUNK):
            fs = slice(f, f + FF_CHUNK)
            us = slice(D_FF + f, D_FF + f + FF_CHUNK)
            gt = jnp.dot(xb, wgu_s[:, fs], preferred_element_type=F32) + bgu_ref[0, :, fs]
            up = jnp.dot(xb, wgu_s[:, us], preferred_element_type=F32) + bgu_ref[0, :, us]
            gt = jnp.minimum(gt, SWIGLU_LIMIT)
            up = jnp.clip(up, -SWIGLU_LIMIT, SWIGLU_LIMIT)
            hg = 0.5 * gt
            act = (up + 1.0) * (hg + hg * jnp.tanh(SWIGLU_ALPHA * hg))
            y = y + jnp.dot(act.astype(BF16), wd_s[fs, :], preferred_element_type=F32)
        _store_packed_rows(ys_ref, y)

    @pl.when(i >= nu_ref[0])
    def _():
        ys_ref[...] = jnp.zeros_like(ys_ref)


def _experts(block_e, n_used, xs, wgu, bgu, wd, bd):
    bm = BM_EXPERT
    nb = xs.shape[0] // (bm * ROW_TILE)
    row = lambda i, be, nu: (jnp.minimum(i, nu[0] - 1), 0)
    orow = lambda i, be, nu: (i, 0)
    wsel = lambda i, be, nu: (be[i], 0, 0)
    return pl.pallas_call(
        _expert_kernel,
        grid_spec=pltpu.PrefetchScalarGridSpec(
            num_scalar_prefetch=2,
            grid=(nb,),
            in_specs=[
                pl.BlockSpec((bm * ROW_TILE, LANES), row),
                pl.BlockSpec((1, D_MODEL, 2 * D_FF), wsel),
                pl.BlockSpec((1, 1, 2 * D_FF), wsel),
                pl.BlockSpec((1, D_FF, D_MODEL), wsel),
                pl.BlockSpec((1, 1, D_MODEL), wsel),
            ],
            out_specs=pl.BlockSpec((bm * ROW_TILE, LANES), orow),
            scratch_shapes=[
                pltpu.VMEM((bm, D_MODEL), BF16),
                pltpu.VMEM((D_MODEL, 2 * D_FF), BF16),
                pltpu.VMEM((D_FF, D_MODEL), BF16),
            ],
        ),
        out_shape=jax.ShapeDtypeStruct(xs.shape, U32),
        compiler_params=pltpu.CompilerParams(
            dimension_semantics=("arbitrary",), vmem_limit_bytes=VMEM_LIMIT),
        name="experts",
    )(block_e, n_used, xs, wgu, bgu, wd, bd)


def _combine_kernel(dest_hbm, ys_hbm, x1_ref, gate_ref, p_ref, gng_ref, wg_ref, wp_ref, png_ref,
                    fng_ref, o_ref, dsm, ybuf, sem_d, sem):
    i = pl.program_id(0)
    tr = x1_ref.shape[0]
    w = TOP_K * tr
    slot = lax.rem(i, 2)

    def dest_copy(step, s):
        return pltpu.make_async_copy(dest_hbm.at[step], dsm.at[pl.ds(pl.multiple_of(s * w, w), w)],
                                     sem_d)

    def row_copy(s, t, k):
        d = dsm[s * w + k * tr + t]
        return pltpu.make_async_copy(ys_hbm.at[_row_tile(d), :], ybuf.at[s, k, _row_tile(t), :],
                                     sem.at[s])

    def gather_tile(step, s):
        cp = dest_copy(step, s)
        cp.start()
        cp.wait()

        def start(t, c):
            for k in range(TOP_K):
                row_copy(s, t, k).start(priority=k % 2)
            return c

        lax.fori_loop(0, tr, start, 0, unroll=2)

    @pl.when(i == 0)
    def _():
        gather_tile(0, 0)

    @pl.when(i + 1 < pl.num_programs(0))
    def _():
        gather_tile(i + 1, 1 - slot)

    def wait(t, c):
        for k in range(TOP_K):
            row_copy(slot, t, k).wait()
        return c

    lax.fori_loop(0, tr, wait, 0, unroll=8)

    lo_chunks, hi_chunks = [], []
    for c in range(ROW_TILE):
        acc_lo = x1_ref[:, c * LANES:(c + 1) * LANES]
        acc_hi = x1_ref[:, HALF + c * LANES:HALF + (c + 1) * LANES]
        for k in range(TOP_K):
            lo, hi = _load_packed_chunk(ybuf.at[slot, k], c, tr)
            g = gate_ref[:, k:k + 1]
            acc_lo = acc_lo + g * lo
            acc_hi = acc_hi + g * hi
        lo_chunks.append(acc_lo)
        hi_chunks.append(acc_hi)
    x2 = jnp.concatenate(lo_chunks + hi_chunks, axis=1)
    gate = jnp.dot(_rms(x2, gng_ref[...]).astype(BF16), wg_ref[...], preferred_element_type=F32)
    gate = 1.0 / (1.0 + jnp.exp(-gate))
    e = jnp.dot(p_ref[...].astype(BF16), wp_ref[...], preferred_element_type=F32)
    x3 = x2 + _rms(e, png_ref[...]) * gate
    o_ref[...] = _rms(x3, fng_ref[...])


def _combine(dest2d, ys, x1, gate_t, p2d, gng, wg, wp, png, fng):
    t = x1.shape[0]
    nt, w = dest2d.shape
    tr = w // TOP_K
    c2 = lambda i: (0, 0)
    return pl.pallas_call(
        _combine_kernel,
        grid=(nt,),
        in_specs=[
            pl.BlockSpec(memory_space=pl.ANY),
            pl.BlockSpec(memory_space=pl.ANY),
            pl.BlockSpec((tr, D_MODEL), lambda i: (i, 0)),
            pl.BlockSpec((tr, TOP_K), lambda i: (i, 0)),
            pl.BlockSpec((tr, PLE_DIM), lambda i: (i, 0)),
            pl.BlockSpec((1, D_MODEL), c2),
            pl.BlockSpec((D_MODEL, D_MODEL), c2),
            pl.BlockSpec((PLE_DIM, D_MODEL), c2),
            pl.BlockSpec((1, D_MODEL), c2),
            pl.BlockSpec((1, D_MODEL), c2),
        ],
        out_specs=pl.BlockSpec((tr, D_MODEL), lambda i: (i, 0)),
        out_shape=jax.ShapeDtypeStruct((t, D_MODEL), F32),
        scratch_shapes=[
            pltpu.SMEM((2 * w,), I32),
            pltpu.VMEM((2, TOP_K, tr * ROW_TILE, LANES), U32),
            pltpu.SemaphoreType.DMA,
            pltpu.SemaphoreType.DMA((2,)),
        ],
        compiler_params=pltpu.CompilerParams(
            dimension_semantics=("arbitrary",), vmem_limit_bytes=VMEM_LIMIT),
        name="combine",
    )(dest2d, ys, x1, gate_t, p2d, gng, wg, wp, png, fng)


def _pad_heads(v):
    return jnp.pad(v.astype(F32), (0, HEAD_PAD - SSD_HEADS)).reshape(1, HEAD_PAD)


def _layer(x, p, mix_norm_g, w_in, conv_w, conv_b, dt_bias, a_log, d_skip, ssd_norm_g, pool_w,
           pool_scale, w_out, ffn_norm_g, router_w, router_b, w_gate_up, b_gate_up, w_down,
           b_down, ple_gate_norm_g, w_ple_gate, w_ple_proj, ple_norm_g, final_g):
    b, l, _ = x.shape
    t = b * l
    x2d = x.reshape(t, D_MODEL)
    row = lambda v: v.reshape(1, -1).astype(F32)

    c_dt = D_SSD + D_CONV
    c_pool = c_dt + SSD_HEADS
    w_cat = jnp.concatenate(
        [w_in[:, :c_dt], w_in[:, c_pool:], w_in[:, c_dt:c_pool],
         jnp.zeros((D_MODEL, HEAD_PAD - SSD_HEADS), w_in.dtype)], axis=1).astype(BF16)
    ii = jnp.arange(CHUNK)
    tri = (ii[None, :] <= ii[:, None]).astype(BF16)
    rr = jnp.arange(CHUNK)[:, None] + POOL_HIST
    jj = jnp.arange(CHUNK + POOL_HIST)[None, :]
    band = jnp.stack([((jj <= rr) & (jj > rr - w)) for w in POOL_WINDOWS]).astype(BF16)
    tr = min(TR_ROUTE, t)
    ri = jnp.arange(tr)
    utri = (ri[:, None] <= ri[None, :]).astype(BF16)

    head_of_col = jnp.arange(D_SSD) // SSD_HEAD_DIM
    expand = (jnp.arange(HEAD_PAD)[:, None] == head_of_col[None, :]).astype(BF16)

    z, xbc, pool_in, dt = _inproj(x2d, row(mix_norm_g), w_cat)
    y_ssd, y_pool = _ssd(xbc.reshape(b, l, D_CONV), z.reshape(b, l, D_SSD),
                         dt.reshape(b, l, HEAD_PAD), conv_w.astype(F32), row(conv_b),
                         _pad_heads(dt_bias), _pad_heads(a_log),
                         row(jnp.repeat(d_skip, SSD_HEAD_DIM)), row(ssd_norm_g), tri, expand,
                         pool_in.reshape(b, l, D_POOL), band, pool_w.astype(BF16), row(pool_scale))
    x1, h, lt = _outproj(y_ssd.reshape(t, D_SSD), y_pool.reshape(t, D_POOL), x2d,
                         w_out[:D_SSD].astype(BF16), w_out[D_SSD:].astype(BF16), row(ffn_norm_g),
                         router_w.T.astype(BF16), router_b.reshape(N_EXPERTS, 1).astype(F32))

    idx, gate, rank, cnt = _route(lt, utri)
    counts = cnt[:, 0].astype(I32)
    bm = BM_EXPERT
    padded = ((counts + bm - 1) // bm) * bm
    pend = jnp.cumsum(padded)
    pstart = (pend - padded).astype(I32)
    nb = (t * TOP_K) // bm + N_EXPERTS
    n_used = (pend[-1] // bm).astype(I32)
    blk = jnp.arange(nb, dtype=I32)
    block_e = jnp.sum((pend[None, :] <= (blk * bm)[:, None]).astype(I32), axis=1)
    block_e = jnp.minimum(block_e, N_EXPERTS - 1)
    last_e = jnp.sum((pend <= (n_used - 1) * bm).astype(I32))
    block_e = jnp.where(blk < n_used, block_e, jnp.minimum(last_e, N_EXPERTS - 1)).astype(I32)

    dest = _dest(pstart, idx, rank)
    nt = t // tr
    dest2d = dest.reshape(nt, TOP_K * tr)
    xs = _dispatch(counts, padded.astype(I32), pstart, n_used.reshape(1), dest2d, h, nb * bm)
    ys = _experts(block_e, n_used.reshape(1), xs, w_gate_up.astype(F32),
                  b_gate_up.reshape(N_EXPERTS, 1, 2 * D_FF).astype(F32), w_down.astype(F32),
                  b_down.reshape(N_EXPERTS, 1, D_MODEL).astype(F32))
    gate_t = gate.transpose(0, 2, 1).reshape(t, TOP_K)
    out = _combine(dest2d, ys, x1, gate_t, p.reshape(t, PLE_DIM), row(ple_gate_norm_g),
                   w_ple_gate.astype(BF16), w_ple_proj.astype(BF16), row(ple_norm_g), row(final_g))
    return out.reshape(b, l, D_MODEL)


def kernel(x, p, mix_norm_g, w_in, conv_w, conv_b, dt_bias, a_log, d_skip, ssd_norm_g, pool_w, pool_scale, w_out, ffn_norm_g, router_w, router_b, w_gate_up, b_gate_up, w_down, b_down, ple_gate_norm_g, w_ple_gate, w_ple_proj, ple_norm_g, final_norm_g):
    assert x.shape[-1] == D_MODEL and mix_norm_g.shape[0] == 1
    layer0 = lambda v: v.reshape(v.shape[1:])
    per_layer = (p, mix_norm_g, w_in, conv_w, conv_b, dt_bias, a_log, d_skip, ssd_norm_g, pool_w,
                 pool_scale, w_out, ffn_norm_g, router_w, router_b, w_gate_up, b_gate_up, w_down,
                 b_down, ple_gate_norm_g, w_ple_gate, w_ple_proj, ple_norm_g)
    return _layer(x, *[layer0(v) for v in per_layer], final_norm_g)
```

```python
import functools

import jax
import jax.numpy as jnp
from jax import lax
from jax.experimental import pallas as pl
from jax.experimental.pallas import tpu as pltpu

F32 = jnp.float32
BF16 = jnp.bfloat16
I32 = jnp.int32

D_MODEL = 1024
SSD_HEADS = 32
SSD_HEAD_DIM = 64
D_SSD = SSD_HEADS * SSD_HEAD_DIM
SSD_GROUPS = 8
HEADS_PER_GROUP = SSD_HEADS // SSD_GROUPS
GROUP_W = HEADS_PER_GROUP * SSD_HEAD_DIM
D_STATE = 128
CONV_WIDTH = 4
CHUNK = 128
D_BC = SSD_GROUPS * D_STATE
D_CONV = D_SSD + 2 * D_BC
POOL_WINDOWS = (2, 4, 8, 16)
POOL_CH = 256
D_POOL = len(POOL_WINDOWS) * POOL_CH
POOL_HIST = 128
N_EXPERTS = 32
TOP_K = 4
D_FF = 1024
SWIGLU_LIMIT = 7.0
SWIGLU_ALPHA = 1.702
PLE_DIM = 256
EPS = 1e-6

LANES = 128
HEAD_PAD = LANES

TM_IN = 512
TM_OUT = 512
TR_ROUTE = 256
DEST_TILES = 16
BM_EXPERT = 512
VMEM_LIMIT = 56 * 1024 * 1024


def _rms(x, g):
    return x * lax.rsqrt(jnp.mean(x * x, axis=-1, keepdims=True) + EPS) * g


def _silu(x):
    h = 0.5 * x
    return h + h * jnp.tanh(h)


HALF = D_MODEL // 2
ROW_TILE = HALF // LANES
U32 = jnp.uint32


def _store_packed_rows(ref, v):
    m = v.shape[0]
    u = lax.bitcast_convert_type(v, U32)
    r = u + jnp.uint32(0x7FFF) + ((u >> 16) & jnp.uint32(1))
    w = (r[:, :HALF] >> 16) | (r[:, HALF:] & jnp.uint32(0xFFFF0000))
    for c in range(ROW_TILE):
        ref[pl.ds(c, m, stride=ROW_TILE), :] = w[:, c * LANES:(c + 1) * LANES]


def _load_packed_chunk(ref, c, m):
    w = ref[pl.ds(c, m, stride=ROW_TILE), :]
    lo = lax.bitcast_convert_type(w << 16, F32)
    hi = lax.bitcast_convert_type(w & jnp.uint32(0xFFFF0000), F32)
    return lo, hi


def _split3(v):
    hi = v.astype(BF16)
    r1 = v - hi.astype(F32)
    mid = r1.astype(BF16)
    lo = (r1 - mid.astype(F32)).astype(BF16)
    return hi, mid, lo


W_MAIN_COLS = D_SSD + D_CONV
W_TAIL_COLS = D_POOL + HEAD_PAD


def _inproj_kernel(x_ref, g_ref, wm_ref, wt_ref, z_ref, xbc_ref, pool_ref, dt_ref):
    hb = _rms(x_ref[...], g_ref[...]).astype(BF16)

    def mm(w_ref, c0, n):
        return jnp.dot(hb, w_ref[:, c0:c0 + n], preferred_element_type=F32)

    cw = 512
    for c in range(0, D_SSD, cw):
        z_ref[:, c:c + cw] = mm(wm_ref, c, cw).astype(BF16)
    for c in range(0, D_CONV, cw):
        xbc_ref[:, c:c + cw] = mm(wm_ref, D_SSD + c, cw).astype(BF16)
    for c in range(0, D_POOL, cw):
        pool_ref[:, c:c + cw] = mm(wt_ref, c, cw).astype(BF16)
    dt_ref[...] = mm(wt_ref, D_POOL, HEAD_PAD)


def _inproj(x2d, g, w_main, w_tail):
    t = x2d.shape[0]
    tm = min(TM_IN, t)
    return pl.pallas_call(
        _inproj_kernel,
        grid=(t // tm,),
        in_specs=[
            pl.BlockSpec((tm, D_MODEL), lambda i: (i, 0)),
            pl.BlockSpec((1, D_MODEL), lambda i: (0, 0)),
            pl.BlockSpec((D_MODEL, W_MAIN_COLS), lambda i: (0, 0), pipeline_mode=pl.Buffered(1)),
            pl.BlockSpec((D_MODEL, W_TAIL_COLS), lambda i: (0, 0), pipeline_mode=pl.Buffered(1)),
        ],
        out_specs=[
            pl.BlockSpec((tm, D_SSD), lambda i: (i, 0)),
            pl.BlockSpec((tm, D_CONV), lambda i: (i, 0)),
            pl.BlockSpec((tm, D_POOL), lambda i: (i, 0)),
            pl.BlockSpec((tm, HEAD_PAD), lambda i: (i, 0)),
        ],
        out_shape=[
            jax.ShapeDtypeStruct((t, D_SSD), BF16),
            jax.ShapeDtypeStruct((t, D_CONV), BF16),
            jax.ShapeDtypeStruct((t, D_POOL), BF16),
            jax.ShapeDtypeStruct((t, HEAD_PAD), F32),
        ],
        compiler_params=pltpu.CompilerParams(
            dimension_semantics=("arbitrary",), vmem_limit_bytes=VMEM_LIMIT),
        name="inproj",
    )(x2d, g, w_main, w_tail)


def _ssd_kernel(xbc_ref, z_ref, dt_ref, convw_ref, convb_ref, dtb_ref, alog_ref, dskip_ref,
                ng_ref, tri_ref, expand_ref, u_ref, band_ref, pw_ref, pscale_ref, y_ref, yp_ref,
                state_ref, hist_ref, xs_s, b_s, c_s, phist_ref):
    q = CHUNK
    li = pl.program_id(1)

    @pl.when(li == 0)
    def _():
        state_ref[...] = jnp.zeros_like(state_ref)
        hist_ref[...] = jnp.zeros_like(hist_ref)
        phist_ref[0:POOL_HIST, :] = jnp.zeros((POOL_HIST, D_POOL), BF16)

    phist_ref[POOL_HIST:POOL_HIST + q, :] = u_ref[0]
    pos = li * q + lax.broadcasted_iota(I32, (q, POOL_CH), 0)
    for gi, w in enumerate(POOL_WINDOWS):
        ps = slice(gi * POOL_CH, (gi + 1) * POOL_CH)
        win_sum = jnp.dot(band_ref[gi], phist_ref[:, ps], preferred_element_type=F32)
        cnt = jnp.minimum(pos + 1, w).astype(F32)
        pre = (win_sum / cnt - u_ref[0, :, ps].astype(F32)).astype(BF16)
        yp = jnp.dot(pre, pw_ref[gi], preferred_element_type=F32) * pscale_ref[:, ps]
        yp_ref[0, :, ps] = yp.astype(BF16)
    phist_ref[0:POOL_HIST, :] = phist_ref[q:q + POOL_HIST, :]

    cw = GROUP_W
    hr = hist_ref.shape[0]
    for ci in range(D_CONV // cw):
        cs = slice(ci * cw, (ci + 1) * cw)
        cur = xbc_ref[0, :, cs].astype(F32)
        xe = jnp.concatenate([hist_ref[:, cs], cur], axis=0)
        hist_ref[:, cs] = cur[q - hr:q, :]
        s1 = pltpu.roll(xe, 1, 0)
        u = convw_ref[3:4, cs] * xe + convw_ref[2:3, cs] * s1
        w = convw_ref[1:2, cs] * xe + convw_ref[0:1, cs] * s1
        acc = convb_ref[:, cs] + u + pltpu.roll(w, 2, 0)
        v = _silu(acc[hr:, :])
        if ci < D_SSD // cw:
            xs_s[ci] = v
        elif ci < (D_SSD + D_BC) // cw:
            j = ci - D_SSD // cw
            b_s[2 * j] = v[:, :D_STATE]
            b_s[2 * j + 1] = v[:, D_STATE:]
        else:
            j = ci - (D_SSD + D_BC) // cw
            c_s[2 * j] = v[:, :D_STATE].astype(BF16)
            c_s[2 * j + 1] = v[:, D_STATE:].astype(BF16)

    xdt = dt_ref[0] + dtb_ref[...]
    dt = jnp.maximum(xdt, 0.0) + jnp.log1p(jnp.exp(-jnp.abs(xdt)))
    a = dt * (-jnp.exp(alog_ref[...]))
    tri = tri_ref[...]
    a_hi, a_mid, a_lo = _split3(a)
    acs = (jnp.dot(tri, a_hi, preferred_element_type=F32)
           + jnp.dot(tri, a_mid, preferred_element_type=F32)
           + jnp.dot(tri, a_lo, preferred_element_type=F32))
    acs_t = acs.T
    dt_t = dt.T
    eacs = jnp.exp(acs)
    w_t = jnp.exp(acs_t[:, q - 1:q] - acs_t) * dt_t
    e_hi = eacs.astype(BF16)
    e_lo = (eacs - e_hi.astype(F32)).astype(BF16)

    row = lax.broadcasted_iota(I32, (q, q), 0)
    col = lax.broadcasted_iota(I32, (q, q), 1)
    causal = col <= row
    head_shift = SSD_HEAD_DIM.bit_length() - 1
    lane_head = jnp.right_shift(lax.broadcasted_iota(I32, (q, GROUP_W), 1), head_shift)
    head_mask = [jnp.where(lane_head == r, 1.0, 0.0).astype(BF16) for r in range(HEADS_PER_GROUP)]

    for g in range(SSD_GROUPS):
        gs = slice(g * GROUP_W, (g + 1) * GROUP_W)
        xs_g = xs_s[g]
        b_g = b_s[g]
        c_g = c_s[g]
        cb = lax.dot_general(c_g, b_g.astype(BF16), (((1,), (1,)), ((), ())),
                             preferred_element_type=F32)
        bt_g = b_g.T
        s_old = state_ref[g]
        y_off = jnp.dot(c_g, s_old.astype(BF16), preferred_element_type=F32)
        esc = (jnp.dot(e_hi, expand_ref[:, gs], preferred_element_type=F32)
               + jnp.dot(e_lo, expand_ref[:, gs], preferred_element_type=F32))
        xs_b = xs_g.astype(BF16)
        m_cat, bts_cat = [], []
        for r in range(HEADS_PER_GROUP):
            h = g * HEADS_PER_GROUP + r
            seg = acs[:, h:h + 1] - acs_t[h:h + 1, :]
            dec = jnp.where(causal, jnp.exp(seg), 0.0)
            m_cat.append((cb * dec * dt_t[h:h + 1, :]).astype(BF16))
            bts_cat.append((bt_g * w_t[h:h + 1, :]).astype(BF16))
        xs_bd = jnp.concatenate([xs_b * head_mask[r] for r in range(HEADS_PER_GROUP)], axis=0)
        y_diag = jnp.dot(jnp.concatenate(m_cat, axis=1), xs_bd, preferred_element_type=F32)
        ds = jnp.dot(jnp.concatenate(bts_cat, axis=1), xs_bd, preferred_element_type=F32)
        state_ref[g] = s_old * esc[q - 1:q, :] + ds
        y = y_diag + y_off * esc + dskip_ref[:, gs] * xs_g
        y = y * _silu(z_ref[0, :, gs].astype(F32))
        y_ref[0, :, gs] = _rms(y, ng_ref[:, gs]).astype(BF16)


def _ssd(xbc, z, dt, conv_w, conv_b, dtb, alog, dskip, ng, tri, expand, u, band, pw, pscale):
    b, l, _ = xbc.shape
    nc = l // CHUNK
    cmap = lambda i, c: (0, 0)
    cmap3 = lambda i, c: (0, 0, 0)
    blk = lambda i, c: (i, c, 0)
    return pl.pallas_call(
        _ssd_kernel,
        grid=(b, nc),
        in_specs=[
            pl.BlockSpec((1, CHUNK, D_CONV), lambda i, c: (i, c, 0)),
            pl.BlockSpec((1, CHUNK, D_SSD), lambda i, c: (i, c, 0)),
            pl.BlockSpec((1, CHUNK, HEAD_PAD), lambda i, c: (i, c, 0)),
            pl.BlockSpec((CONV_WIDTH, D_CONV), cmap),
            pl.BlockSpec((1, D_CONV), cmap),
            pl.BlockSpec((1, HEAD_PAD), cmap),
            pl.BlockSpec((1, HEAD_PAD), cmap),
            pl.BlockSpec((1, D_SSD), cmap),
            pl.BlockSpec((1, D_SSD), cmap),
            pl.BlockSpec((CHUNK, CHUNK), cmap),
            pl.BlockSpec((HEAD_PAD, D_SSD), cmap),
            pl.BlockSpec((1, CHUNK, D_POOL), blk),
            pl.BlockSpec((len(POOL_WINDOWS), CHUNK, CHUNK + POOL_HIST), cmap3),
            pl.BlockSpec((len(POOL_WINDOWS), POOL_CH, POOL_CH), cmap3),
            pl.BlockSpec((1, D_POOL), cmap),
        ],
        out_specs=[pl.BlockSpec((1, CHUNK, D_SSD), blk), pl.BlockSpec((1, CHUNK, D_POOL), blk)],
        out_shape=[jax.ShapeDtypeStruct((b, l, D_SSD), BF16),
                   jax.ShapeDtypeStruct((b, l, D_POOL), BF16)],
        scratch_shapes=[
            pltpu.VMEM((SSD_GROUPS, D_STATE, GROUP_W), F32),
            pltpu.VMEM((8, D_CONV), F32),
            pltpu.VMEM((SSD_GROUPS, CHUNK, GROUP_W), F32),
            pltpu.VMEM((SSD_GROUPS, CHUNK, D_STATE), F32),
            pltpu.VMEM((SSD_GROUPS, CHUNK, D_STATE), BF16),
            pltpu.VMEM((POOL_HIST + CHUNK, D_POOL), BF16),
        ],
        compiler_params=pltpu.CompilerParams(
            dimension_semantics=("arbitrary", "arbitrary"), vmem_limit_bytes=VMEM_LIMIT),
        name="ssd",
    )(xbc, z, dt, conv_w, conv_b, dtb, alog, dskip, ng, tri, expand, u, band, pw, pscale)


def _outproj_kernel(ys_ref, yp_ref, x_ref, ws_ref, wp_ref, g_ref, rwt_ref, rb_ref,
                    x1_ref, h_ref, lt_ref):
    acc = jnp.dot(ys_ref[...], ws_ref[...], preferred_element_type=F32)
    acc = acc + jnp.dot(yp_ref[...], wp_ref[...], preferred_element_type=F32)
    x1 = x_ref[...] + acc
    x1_ref[...] = x1
    h = _rms(x1, g_ref[...])
    _store_packed_rows(h_ref, h)
    lt = lax.dot_general(rwt_ref[...], h.astype(BF16), (((1,), (1,)), ((), ())),
                         preferred_element_type=F32)
    lt_ref[...] = lt + rb_ref[...]


def _outproj(y_ssd, y_pool, x2d, w_s, w_p, g, rwt, rb):
    t = x2d.shape[0]
    tm = min(TM_OUT, t)
    c2 = lambda i: (0, 0)
    return pl.pallas_call(
        _outproj_kernel,
        grid=(t // tm,),
        in_specs=[
            pl.BlockSpec((tm, D_SSD), lambda i: (i, 0)),
            pl.BlockSpec((tm, D_POOL), lambda i: (i, 0)),
            pl.BlockSpec((tm, D_MODEL), lambda i: (i, 0)),
            pl.BlockSpec((D_SSD, D_MODEL), c2),
            pl.BlockSpec((D_POOL, D_MODEL), c2),
            pl.BlockSpec((1, D_MODEL), c2),
            pl.BlockSpec((N_EXPERTS, D_MODEL), c2),
            pl.BlockSpec((N_EXPERTS, 1), c2),
        ],
        out_specs=[
            pl.BlockSpec((tm, D_MODEL), lambda i: (i, 0)),
            pl.BlockSpec((tm * ROW_TILE, LANES), lambda i: (i, 0)),
            pl.BlockSpec((N_EXPERTS, tm), lambda i: (0, i)),
        ],
        out_shape=[
            jax.ShapeDtypeStruct((t, D_MODEL), F32),
            jax.ShapeDtypeStruct((t * ROW_TILE, LANES), U32),
            jax.ShapeDtypeStruct((N_EXPERTS, t), F32),
        ],
        compiler_params=pltpu.CompilerParams(
            dimension_semantics=("arbitrary",), vmem_limit_bytes=VMEM_LIMIT),
        name="outproj",
    )(y_ssd, y_pool, x2d, w_s, w_p, g, rwt, rb)


def _route_kernel(lt_ref, utri_ref, idx_ref, gate_ref, rank_ref, cnt_ref, carry_ref):
    tr = lt_ref.shape[1]

    @pl.when(pl.program_id(0) == 0)
    def _():
        carry_ref[...] = jnp.zeros_like(carry_ref)

    l = lt_ref[...]
    eidx = lax.broadcasted_iota(I32, (N_EXPERTS, tr), 0).astype(F32)
    sels, vals, idxs = [], [], []
    for _ in range(TOP_K):
        m = jnp.max(l, axis=0, keepdims=True)
        ik = jnp.min(jnp.where(l == m, eidx, float(N_EXPERTS)), axis=0, keepdims=True)
        sel = eidx == ik
        l = jnp.where(sel, -jnp.inf, l)
        sels.append(sel)
        vals.append(m)
        idxs.append(ik)
    es = [jnp.exp(v - vals[0]) for v in vals]
    den = es[0] + es[1] + es[2] + es[3]
    multi = jnp.zeros((N_EXPERTS, tr), F32)
    for sel in sels:
        multi = multi + jnp.where(sel, 1.0, 0.0)
    incl = jnp.dot(multi.astype(BF16), utri_ref[...], preferred_element_type=F32)
    excl = incl - multi + carry_ref[:, 0:1]
    for k in range(TOP_K):
        idx_ref[0, k:k + 1, :] = idxs[k].astype(I32)
        gate_ref[0, k:k + 1, :] = es[k] / den
        rk = jnp.sum(jnp.where(sels[k], excl, 0.0), axis=0, keepdims=True)
        rank_ref[0, k:k + 1, :] = rk.astype(I32)
    carry_ref[...] = carry_ref[...] + jnp.sum(multi, axis=1, keepdims=True)
    cnt_ref[...] = carry_ref[...]


def _route(lt, utri):
    t = lt.shape[1]
    tr = utri.shape[0]
    nt = t // tr
    o3 = lambda i: (i, 0, 0)
    return pl.pallas_call(
        _route_kernel,
        grid=(nt,),
        in_specs=[
            pl.BlockSpec((N_EXPERTS, tr), lambda i: (0, i)),
            pl.BlockSpec((tr, tr), lambda i: (0, 0)),
        ],
        out_specs=[
            pl.BlockSpec((1, TOP_K, tr), o3),
            pl.BlockSpec((1, TOP_K, tr), o3),
            pl.BlockSpec((1, TOP_K, tr), o3),
            pl.BlockSpec((N_EXPERTS, LANES), lambda i: (0, 0)),
        ],
        out_shape=[
            jax.ShapeDtypeStruct((nt, TOP_K, tr), I32),
            jax.ShapeDtypeStruct((nt, TOP_K, tr), F32),
            jax.ShapeDtypeStruct((nt, TOP_K, tr), I32),
            jax.ShapeDtypeStruct((N_EXPERTS, LANES), F32),
        ],
        scratch_shapes=[pltpu.VMEM((N_EXPERTS, LANES), F32)],
        compiler_params=pltpu.CompilerParams(
            dimension_semantics=("arbitrary",), vmem_limit_bytes=VMEM_LIMIT),
        name="route",
    )(lt, utri)


def _dest_kernel(pstart_ref, idx_ref, rank_ref, dest_ref):
    idx = idx_ref[...]
    base = jnp.zeros_like(idx)
    for e in range(N_EXPERTS):
        base = jnp.where(idx == e, pstart_ref[e], base)
    dest_ref[...] = base + rank_ref[...]


def _dest(pstart, idx, rank):
    nt, _, tr = idx.shape
    tb = min(DEST_TILES, nt)
    o3 = lambda i, ps: (i, 0, 0)
    return pl.pallas_call(
        _dest_kernel,
        grid_spec=pltpu.PrefetchScalarGridSpec(
            num_scalar_prefetch=1,
            grid=(nt // tb,),
            in_specs=[pl.BlockSpec((tb, TOP_K, tr), o3), pl.BlockSpec((tb, TOP_K, tr), o3)],
            out_specs=pl.BlockSpec((tb, TOP_K, tr), o3),
        ),
        out_shape=jax.ShapeDtypeStruct((nt, TOP_K, tr), I32),
        compiler_params=pltpu.CompilerParams(dimension_semantics=("arbitrary",)),
        name="dest",
    )(pstart, idx, rank)


def _row_tile(r):
    return pl.ds(pl.multiple_of(r * ROW_TILE, ROW_TILE), ROW_TILE)


def _dispatch_kernel(cnt_ref, pad_ref, pstart_ref, nu_ref, dest_hbm, h_ref, xs_hbm, dsm, zrow, sem_d,
                     sem, sem_z):
    i = pl.program_id(0)
    w = dsm.shape[0] // 2
    tr = w // TOP_K
    zrows = zrow.shape[0]
    nb = xs_hbm.shape[0] // zrows
    slot = lax.rem(i, 2)

    def dest_copy(step, s):
        return pltpu.make_async_copy(dest_hbm.at[step], dsm.at[pl.ds(pl.multiple_of(s * w, w), w)],
                                     sem_d)

    def zero_copy(r):
        return pltpu.make_async_copy(zrow.at[pl.ds(0, ROW_TILE), :], xs_hbm.at[_row_tile(r), :], sem)

    def zero_block_copy(blk):
        rows = pl.ds(pl.multiple_of(blk * zrows, zrows), zrows)
        return pltpu.make_async_copy(zrow, xs_hbm.at[rows, :], sem_z)

    @pl.when(i == 0)
    def _():
        dest_copy(0, 0).start()
        zrow[...] = jnp.zeros_like(zrow)

        def zstart(blk, c):
            zero_block_copy(blk).start()
            return c

        def zwait(blk, c):
            zero_block_copy(blk).wait()
            return c

        lax.fori_loop(nu_ref[0], nb, zstart, 0)
        lax.fori_loop(nu_ref[0], nb, zwait, 0)
        for e in range(N_EXPERTS):
            lo, hi, base = cnt_ref[e], pad_ref[e], pstart_ref[e]

            def start(j, c, base=base):
                zero_copy(base + j).start()
                return c

            def wait(j, c, base=base):
                zero_copy(base + j).wait()
                return c

            lax.fori_loop(lo, hi, start, 0)
            lax.fori_loop(lo, hi, wait, 0)

    dest_copy(i, slot).wait()

    @pl.when(i + 1 < pl.num_programs(0))
    def _():
        dest_copy(i + 1, 1 - slot).start()

    def row_copy(t, k):
        d = dsm[slot * w + k * tr + t]
        return pltpu.make_async_copy(h_ref.at[_row_tile(t), :], xs_hbm.at[_row_tile(d), :], sem)

    def start(t, c):
        for k in range(TOP_K):
            row_copy(t, k).start(priority=k % 2)
        return c

    def wait(t, c):
        for k in range(TOP_K):
            row_copy(t, k).wait()
        return c

    lax.fori_loop(0, tr, start, 0, unroll=2)
    lax.fori_loop(0, tr, wait, 0, unroll=8)


def _dispatch(counts, padded, pstart, n_used, dest2d, h, n_slots):
    nt, w = dest2d.shape
    tr = w // TOP_K
    return pl.pallas_call(
        _dispatch_kernel,
        grid_spec=pltpu.PrefetchScalarGridSpec(
            num_scalar_prefetch=4,
            grid=(nt,),
            in_specs=[
                pl.BlockSpec(memory_space=pl.ANY),
                pl.BlockSpec((tr * ROW_TILE, LANES), lambda i, *_: (i, 0)),
            ],
            out_specs=pl.BlockSpec(memory_space=pl.ANY),
            scratch_shapes=[
                pltpu.SMEM((2 * w,), I32),
                pltpu.VMEM((BM_EXPERT * ROW_TILE, LANES), U32),
                pltpu.SemaphoreType.DMA,
                pltpu.SemaphoreType.DMA,
                pltpu.SemaphoreType.DMA,
            ],
        ),
        out_shape=jax.ShapeDtypeStruct((n_slots * ROW_TILE, LANES), U32),
        compiler_params=pltpu.CompilerParams(dimension_semantics=("arbitrary",)),
        name="dispatch",
    )(counts, padded, pstart, n_used, dest2d, h)


def _expert_kernel(be_ref, nu_ref, xs_ref, wgu_ref, bgu_ref, wd_ref, bd_ref, ys_ref, xb_s, wgu_s,
                   wd_s):
    i = pl.program_id(0)
    bm = xb_s.shape[0]

    @pl.when((i == 0) | (be_ref[i] != be_ref[jnp.maximum(i - 1, 0)]))
    def _():
        rc = 128
        for r in range(0, D_MODEL, rc):
            wgu_s[r:r + rc, :] = wgu_ref[0, r:r + rc, :].astype(BF16)
        for r in range(0, D_FF, rc):
            wd_s[r:r + rc, :] = wd_ref[0, r:r + rc, :].astype(BF16)

    @pl.when(i < nu_ref[0])
    def _():
        for c in range(ROW_TILE):
            lo, hi = _load_packed_chunk(xs_ref, c, bm)
            xb_s[:, c * LANES:(c + 1) * LANES] = lo.astype(BF16)
            xb_s[:, HALF + c * LANES:HALF + (c + 1) * LANES] = hi.astype(BF16)
        gu = jnp.dot(xb_s[...], wgu_s[...], preferred_element_type=F32) + bgu_ref[0]
        gt = jnp.minimum(gu[:, :D_FF], SWIGLU_LIMIT)
        up = jnp.clip(gu[:, D_FF:], -SWIGLU_LIMIT, SWIGLU_LIMIT)
        hg = 0.5 * gt
        act = (up + 1.0) * (hg + hg * jnp.tanh(SWIGLU_ALPHA * hg))
        y = jnp.dot(act.astype(BF16), wd_s[...], preferred_element_type=F32) + bd_ref[0]
        _store_packed_rows(ys_ref, y)

    @pl.when(i >= nu_ref[0])
    def _():
        ys_ref[...] = jnp.zeros_like(ys_ref)


def _experts(block_e, n_used, xs, wgu, bgu, wd, bd):
    bm = BM_EXPERT
    nb = xs.shape[0] // (bm * ROW_TILE)
    row = lambda i, be, nu: (jnp.minimum(i, nu[0] - 1), 0)
    orow = lambda i, be, nu: (i, 0)
    wsel = lambda i, be, nu: (be[i], 0, 0)
    return pl.pallas_call(
        _expert_kernel,
        grid_spec=pltpu.PrefetchScalarGridSpec(
            num_scalar_prefetch=2,
            grid=(nb,),
            in_specs=[
                pl.BlockSpec((bm * ROW_TILE, LANES), row),
                pl.BlockSpec((1, D_MODEL, 2 * D_FF), wsel),
                pl.BlockSpec((1, 1, 2 * D_FF), wsel),
                pl.BlockSpec((1, D_FF, D_MODEL), wsel),
                pl.BlockSpec((1, 1, D_MODEL), wsel),
            ],
            out_specs=pl.BlockSpec((bm * ROW_TILE, LANES), orow),
            scratch_shapes=[
                pltpu.VMEM((bm, D_MODEL), BF16),
                pltpu.VMEM((D_MODEL, 2 * D_FF), BF16),
                pltpu.VMEM((D_FF, D_MODEL), BF16),
            ],
        ),
        out_shape=jax.ShapeDtypeStruct(xs.shape, U32),
        compiler_params=pltpu.CompilerParams(
            dimension_semantics=("arbitrary",), vmem_limit_bytes=VMEM_LIMIT),
        name="experts",
    )(block_e, n_used, xs, wgu, bgu, wd, bd)


def _combine_kernel(dest_hbm, ys_hbm, x1_ref, gate_ref, p_ref, gng_ref, wg_ref, wp_ref, png_ref,
                    fng_ref, o_ref, dsm, ybuf, sem_d, sem):
    i = pl.program_id(0)
    tr = x1_ref.shape[0]
    w = TOP_K * tr
    slot = lax.rem(i, 2)

    def dest_copy(step, s):
        return pltpu.make_async_copy(dest_hbm.at[step], dsm.at[pl.ds(pl.multiple_of(s * w, w), w)],
                                     sem_d)

    def row_copy(s, t, k):
        d = dsm[s * w + k * tr + t]
        return pltpu.make_async_copy(ys_hbm.at[_row_tile(d), :], ybuf.at[s, k, _row_tile(t), :],
                                     sem.at[s])

    def gather_tile(step, s):
        cp = dest_copy(step, s)
        cp.start()
        cp.wait()

        def start(t, c):
            for k in range(TOP_K):
                row_copy(s, t, k).start(priority=k % 2)
            return c

        lax.fori_loop(0, tr, start, 0, unroll=2)

    @pl.when(i == 0)
    def _():
        gather_tile(0, 0)

    @pl.when(i + 1 < pl.num_programs(0))
    def _():
        gather_tile(i + 1, 1 - slot)

    def wait(t, c):
        for k in range(TOP_K):
            row_copy(slot, t, k).wait()
        return c

    lax.fori_loop(0, tr, wait, 0, unroll=8)

    lo_chunks, hi_chunks = [], []
    for c in range(ROW_TILE):
        acc_lo = x1_ref[:, c * LANES:(c + 1) * LANES]
        acc_hi = x1_ref[:, HALF + c * LANES:HALF + (c + 1) * LANES]
        for k in range(TOP_K):
            lo, hi = _load_packed_chunk(ybuf.at[slot, k], c, tr)
            g = gate_ref[:, k:k + 1]
            acc_lo = acc_lo + g * lo
            acc_hi = acc_hi + g * hi
        lo_chunks.append(acc_lo)
        hi_chunks.append(acc_hi)
    x2 = jnp.concatenate(lo_chunks + hi_chunks, axis=1)
    gate = jnp.dot(_rms(x2, gng_ref[...]).astype(BF16), wg_ref[...], preferred_element_type=F32)
    gate = 1.0 / (1.0 + jnp.exp(-gate))
    e = jnp.dot(p_ref[...].astype(BF16), wp_ref[...], preferred_element_type=F32)
    x3 = x2 + _rms(e, png_ref[...]) * gate
    o_ref[...] = _rms(x3, fng_ref[...])


def _combine(dest2d, ys, x1, gate_t, p2d, gng, wg, wp, png, fng):
    t = x1.shape[0]
    nt, w = dest2d.shape
    tr = w // TOP_K
    c2 = lambda i: (0, 0)
    return pl.pallas_call(
        _combine_kernel,
        grid=(nt,),
        in_specs=[
            pl.BlockSpec(memory_space=pl.ANY),
            pl.BlockSpec(memory_space=pl.ANY),
            pl.BlockSpec((tr, D_MODEL), lambda i: (i, 0)),
            pl.BlockSpec((tr, TOP_K), lambda i: (i, 0)),
            pl.BlockSpec((tr, PLE_DIM), lambda i: (i, 0)),
            pl.BlockSpec((1, D_MODEL), c2),
            pl.BlockSpec((D_MODEL, D_MODEL), c2),
            pl.BlockSpec((PLE_DIM, D_MODEL), c2),
            pl.BlockSpec((1, D_MODEL), c2),
            pl.BlockSpec((1, D_MODEL), c2),
        ],
        out_specs=pl.BlockSpec((tr, D_MODEL), lambda i: (i, 0)),
        out_shape=jax.ShapeDtypeStruct((t, D_MODEL), F32),
        scratch_shapes=[
            pltpu.SMEM((2 * w,), I32),
            pltpu.VMEM((2, TOP_K, tr * ROW_TILE, LANES), U32),
            pltpu.SemaphoreType.DMA,
            pltpu.SemaphoreType.DMA((2,)),
        ],
        compiler_params=pltpu.CompilerParams(
            dimension_semantics=("arbitrary",), vmem_limit_bytes=VMEM_LIMIT),
        name="combine",
    )(dest2d, ys, x1, gate_t, p2d, gng, wg, wp, png, fng)


def _pad_heads(v):
    return jnp.pad(v.astype(F32), (0, HEAD_PAD - SSD_HEADS)).reshape(1, HEAD_PAD)


def _layer(x, p, mix_norm_g, w_in, conv_w, conv_b, dt_bias, a_log, d_skip, ssd_norm_g, pool_w,
           pool_scale, w_out, ffn_norm_g, router_w, router_b, w_gate_up, b_gate_up, w_down,
           b_down, ple_gate_norm_g, w_ple_gate, w_ple_proj, ple_norm_g, final_g):
    b, l, _ = x.shape
    t = b * l
    x2d = x.reshape(t, D_MODEL)
    row = lambda v: v.reshape(1, -1).astype(F32)

    c_dt = D_SSD + D_CONV
    c_pool = c_dt + SSD_HEADS
    w_main = w_in[:, :c_dt].astype(BF16)
    w_tail = jnp.concatenate(
        [w_in[:, c_pool:], w_in[:, c_dt:c_pool],
         jnp.zeros((D_MODEL, HEAD_PAD - SSD_HEADS), w_in.dtype)], axis=1).astype(BF16)
    ii = jnp.arange(CHUNK)
    tri = (ii[None, :] <= ii[:, None]).astype(BF16)
    rr = jnp.arange(CHUNK)[:, None] + POOL_HIST
    jj = jnp.arange(CHUNK + POOL_HIST)[None, :]
    band = jnp.stack([((jj <= rr) & (jj > rr - w)) for w in POOL_WINDOWS]).astype(BF16)
    tr = min(TR_ROUTE, t)
    ri = jnp.arange(tr)
    utri = (ri[:, None] <= ri[None, :]).astype(BF16)

    head_of_col = jnp.arange(D_SSD) // SSD_HEAD_DIM
    expand = (jnp.arange(HEAD_PAD)[:, None] == head_of_col[None, :]).astype(BF16)

    z, xbc, pool_in, dt = _inproj(x2d, row(mix_norm_g), w_main, w_tail)
    y_ssd, y_pool = _ssd(xbc.reshape(b, l, D_CONV), z.reshape(b, l, D_SSD),
                         dt.reshape(b, l, HEAD_PAD), conv_w.astype(F32), row(conv_b),
                         _pad_heads(dt_bias), _pad_heads(a_log),
                         row(jnp.repeat(d_skip, SSD_HEAD_DIM)), row(ssd_norm_g), tri, expand,
                         pool_in.reshape(b, l, D_POOL), band, pool_w.astype(BF16), row(pool_scale))
    x1, h, lt = _outproj(y_ssd.reshape(t, D_SSD), y_pool.reshape(t, D_POOL), x2d,
                         w_out[:D_SSD].astype(BF16), w_out[D_SSD:].astype(BF16), row(ffn_norm_g),
                         router_w.T.astype(BF16), router_b.reshape(N_EXPERTS, 1).astype(F32))

    idx, gate, rank, cnt = _route(lt, utri)
    counts = cnt[:, 0].astype(I32)
    bm = BM_EXPERT
    padded = ((counts + bm - 1) // bm) * bm
    pend = jnp.cumsum(padded)
    pstart = (pend - padded).astype(I32)
    nb = (t * TOP_K) // bm + N_EXPERTS
    n_used = (pend[-1] // bm).astype(I32)
    blk = jnp.arange(nb, dtype=I32)
    block_e = jnp.sum((pend[None, :] <= (blk * bm)[:, None]).astype(I32), axis=1)
    block_e = jnp.minimum(block_e, N_EXPERTS - 1)
    last_e = jnp.sum((pend <= (n_used - 1) * bm).astype(I32))
    block_e = jnp.where(blk < n_used, block_e, jnp.minimum(last_e, N_EXPERTS - 1)).astype(I32)

    dest = _dest(pstart, idx, rank)
    nt = t // tr
    dest2d = dest.reshape(nt, TOP_K * tr)
    xs = _dispatch(counts, padded.astype(I32), pstart, n_used.reshape(1), dest2d, h, nb * bm)
    ys = _experts(block_e, n_used.reshape(1), xs, w_gate_up.astype(F32),
                  b_gate_up.reshape(N_EXPERTS, 1, 2 * D_FF).astype(F32), w_down.astype(F32),
                  b_down.reshape(N_EXPERTS, 1, D_MODEL).astype(F32))
    gate_t = gate.transpose(0, 2, 1).reshape(t, TOP_K)
    out = _combine(dest2d, ys, x1, gate_t, p.reshape(t, PLE_DIM), row(ple_gate_norm_g),
                   w_ple_gate.astype(BF16), w_ple_proj.astype(BF16), row(ple_norm_g), row(final_g))
    return out.reshape(b, l, D_MODEL)


def kernel(x, p, mix_norm_g, w_in, conv_w, conv_b, dt_bias, a_log, d_skip, ssd_norm_g, pool_w, pool_scale, w_out, ffn_norm_g, router_w, router_b, w_gate_up, b_gate_up, w_down, b_down, ple_gate_norm_g, w_ple_gate, w_ple_proj, ple_norm_g, final_norm_g):
    assert x.shape[-1] == D_MODEL and mix_norm_g.shape[0] == 1
    layer0 = lambda v: v.reshape(v.shape[1:])
    per_layer = (p, mix_norm_g, w_in, conv_w, conv_b, dt_bias, a_log, d_skip, ssd_norm_g, pool_w,
                 pool_scale, w_out, ffn_norm_g, router_w, router_b, w_gate_up, b_gate_up, w_down,
                 b_down, ple_gate_norm_g, w_ple_gate, w_ple_proj, ple_norm_g)
    return _layer(x, *[layer0(v) for v in per_layer], final_norm_g)
```

```python
import functools

import jax
import jax.numpy as jnp
from jax import lax
from jax.experimental import pallas as pl
from jax.experimental.pallas import tpu as pltpu

F32 = jnp.float32
BF16 = jnp.bfloat16
I32 = jnp.int32

D_MODEL = 1024
SSD_HEADS = 32
SSD_HEAD_DIM = 64
D_SSD = SSD_HEADS * SSD_HEAD_DIM
SSD_GROUPS = 8
HEADS_PER_GROUP = SSD_HEADS // SSD_GROUPS
GROUP_W = HEADS_PER_GROUP * SSD_HEAD_DIM
D_STATE = 128
CONV_WIDTH = 4
CHUNK = 128
D_BC = SSD_GROUPS * D_STATE
D_CONV = D_SSD + 2 * D_BC
POOL_WINDOWS = (2, 4, 8, 16)
POOL_CH = 256
D_POOL = len(POOL_WINDOWS) * POOL_CH
POOL_HIST = 128
N_EXPERTS = 32
TOP_K = 4
D_FF = 1024
SWIGLU_LIMIT = 7.0
SWIGLU_ALPHA = 1.702
PLE_DIM = 256
EPS = 1e-6

LANES = 128
HEAD_PAD = LANES

TM_IN = 512
SSD_STEP_CHUNKS = 2
TM_OUT = 512
TR_ROUTE = 256
DEST_TILES = 16
BM_EXPERT = 512
VMEM_LIMIT = 56 * 1024 * 1024


def _rms(x, g):
    return x * lax.rsqrt(jnp.mean(x * x, axis=-1, keepdims=True) + EPS) * g


def _silu(x):
    h = 0.5 * x
    return h + h * jnp.tanh(h)


HALF = D_MODEL // 2
ROW_TILE = HALF // LANES
U32 = jnp.uint32


def _store_packed_rows(ref, v):
    m = v.shape[0]
    u = lax.bitcast_convert_type(v, U32)
    r = u + jnp.uint32(0x7FFF) + ((u >> 16) & jnp.uint32(1))
    w = (r[:, :HALF] >> 16) | (r[:, HALF:] & jnp.uint32(0xFFFF0000))
    for c in range(ROW_TILE):
        ref[pl.ds(c, m, stride=ROW_TILE), :] = w[:, c * LANES:(c + 1) * LANES]


def _load_packed_chunk(ref, c, m):
    w = ref[pl.ds(c, m, stride=ROW_TILE), :]
    lo = lax.bitcast_convert_type(w << 16, F32)
    hi = lax.bitcast_convert_type(w & jnp.uint32(0xFFFF0000), F32)
    return lo, hi


def _split3(v):
    hi = v.astype(BF16)
    r1 = v - hi.astype(F32)
    mid = r1.astype(BF16)
    lo = (r1 - mid.astype(F32)).astype(BF16)
    return hi, mid, lo


D_IN = D_SSD + D_CONV + SSD_HEADS + D_POOL
W_TAIL_COLS = D_POOL + HEAD_PAD


def _inproj_kernel(x_ref, g_ref, wm_ref, wt_ref, z_ref, xbc_ref, pool_ref, dt_ref):
    hb = _rms(x_ref[...], g_ref[...]).astype(BF16)

    def mm(w_ref, c0, n):
        return jnp.dot(hb, w_ref[:, c0:c0 + n], preferred_element_type=F32)

    cw = 512
    for c in range(0, D_SSD, cw):
        z_ref[:, c:c + cw] = mm(wm_ref, c, cw).astype(BF16)
    for c in range(0, D_CONV, cw):
        xbc_ref[:, c:c + cw] = mm(wm_ref, D_SSD + c, cw).astype(BF16)
    for c in range(0, D_POOL, cw):
        pool_ref[:, c:c + cw] = mm(wt_ref, c, cw).astype(BF16)
    dt_ref[...] = mm(wt_ref, D_POOL, HEAD_PAD)


def _inproj(x2d, g, w_main, w_tail):
    t = x2d.shape[0]
    tm = min(TM_IN, t)
    return pl.pallas_call(
        _inproj_kernel,
        grid=(t // tm,),
        in_specs=[
            pl.BlockSpec((tm, D_MODEL), lambda i: (i, 0)),
            pl.BlockSpec((1, D_MODEL), lambda i: (0, 0)),
            pl.BlockSpec((D_MODEL, D_IN), lambda i: (0, 0), pipeline_mode=pl.Buffered(1)),
            pl.BlockSpec((D_MODEL, W_TAIL_COLS), lambda i: (0, 0), pipeline_mode=pl.Buffered(1)),
        ],
        out_specs=[
            pl.BlockSpec((tm, D_SSD), lambda i: (i, 0)),
            pl.BlockSpec((tm, D_CONV), lambda i: (i, 0)),
            pl.BlockSpec((tm, D_POOL), lambda i: (i, 0)),
            pl.BlockSpec((tm, HEAD_PAD), lambda i: (i, 0)),
        ],
        out_shape=[
            jax.ShapeDtypeStruct((t, D_SSD), BF16),
            jax.ShapeDtypeStruct((t, D_CONV), BF16),
            jax.ShapeDtypeStruct((t, D_POOL), BF16),
            jax.ShapeDtypeStruct((t, HEAD_PAD), F32),
        ],
        compiler_params=pltpu.CompilerParams(
            dimension_semantics=("arbitrary",), vmem_limit_bytes=VMEM_LIMIT),
        name="inproj",
    )(x2d, g, w_main, w_tail)


def _ssd_kernel(xbc_ref, z_ref, dt_ref, convw_ref, convb_ref, dtb_ref, alog_ref, dskip_ref,
                ng_ref, tri_ref, expand_ref, u_ref, band_ref, pw_ref, pscale_ref, y_ref, yp_ref,
                state_ref, hist_ref, xs_s, b_s, c_s, phist_ref):
    q = CHUNK
    n_sub = xbc_ref.shape[1] // q

    @pl.when(pl.program_id(1) == 0)
    def _():
        state_ref[...] = jnp.zeros_like(state_ref)
        hist_ref[...] = jnp.zeros_like(hist_ref)
        phist_ref[0:POOL_HIST, :] = jnp.zeros((POOL_HIST, D_POOL), BF16)

    def one_chunk(sub, carry):
        rows = pl.ds(pl.multiple_of(sub * q, q), q)
        li = pl.program_id(1) * n_sub + sub

        phist_ref[POOL_HIST:POOL_HIST + q, :] = u_ref[0, rows, :]
        pos = li * q + lax.broadcasted_iota(I32, (q, POOL_CH), 0)
        for gi, w in enumerate(POOL_WINDOWS):
            ps = slice(gi * POOL_CH, (gi + 1) * POOL_CH)
            win_sum = jnp.dot(band_ref[gi], phist_ref[:, ps], preferred_element_type=F32)
            cnt = jnp.minimum(pos + 1, w).astype(F32)
            pre = (win_sum / cnt - u_ref[0, rows, ps].astype(F32)).astype(BF16)
            yp = jnp.dot(pre, pw_ref[gi], preferred_element_type=F32) * pscale_ref[:, ps]
            yp_ref[0, rows, ps] = yp.astype(BF16)
        phist_ref[0:POOL_HIST, :] = phist_ref[q:q + POOL_HIST, :]

        cw = GROUP_W
        hr = hist_ref.shape[0]
        for ci in range(D_CONV // cw):
            cs = slice(ci * cw, (ci + 1) * cw)
            cur = xbc_ref[0, rows, cs].astype(F32)
            xe = jnp.concatenate([hist_ref[:, cs], cur], axis=0)
            hist_ref[:, cs] = cur[q - hr:q, :]
            s1 = pltpu.roll(xe, 1, 0)
            u = convw_ref[3:4, cs] * xe + convw_ref[2:3, cs] * s1
            w = convw_ref[1:2, cs] * xe + convw_ref[0:1, cs] * s1
            acc = convb_ref[:, cs] + u + pltpu.roll(w, 2, 0)
            v = _silu(acc[hr:, :])
            if ci < D_SSD // cw:
                xs_s[ci] = v
            elif ci < (D_SSD + D_BC) // cw:
                j = ci - D_SSD // cw
                b_s[2 * j] = v[:, :D_STATE]
                b_s[2 * j + 1] = v[:, D_STATE:]
            else:
                j = ci - (D_SSD + D_BC) // cw
                c_s[2 * j] = v[:, :D_STATE].astype(BF16)
                c_s[2 * j + 1] = v[:, D_STATE:].astype(BF16)

        xdt = dt_ref[0, rows, :] + dtb_ref[...]
        dt = jnp.maximum(xdt, 0.0) + jnp.log1p(jnp.exp(-jnp.abs(xdt)))
        a = dt * (-jnp.exp(alog_ref[...]))
        tri = tri_ref[...]
        a_hi, a_mid, a_lo = _split3(a)
        acs = (jnp.dot(tri, a_hi, preferred_element_type=F32)
               + jnp.dot(tri, a_mid, preferred_element_type=F32)
               + jnp.dot(tri, a_lo, preferred_element_type=F32))
        acs_t = acs.T
        dt_t = dt.T
        eacs = jnp.exp(acs)
        w_t = jnp.exp(acs_t[:, q - 1:q] - acs_t) * dt_t
        e_hi = eacs.astype(BF16)
        e_lo = (eacs - e_hi.astype(F32)).astype(BF16)

        row = lax.broadcasted_iota(I32, (q, q), 0)
        col = lax.broadcasted_iota(I32, (q, q), 1)
        causal = col <= row
        head_shift = SSD_HEAD_DIM.bit_length() - 1
        lane_head = jnp.right_shift(lax.broadcasted_iota(I32, (q, GROUP_W), 1), head_shift)
        head_mask = [jnp.where(lane_head == r, 1.0, 0.0).astype(BF16)
                     for r in range(HEADS_PER_GROUP)]

        for g in range(SSD_GROUPS):
            gs = slice(g * GROUP_W, (g + 1) * GROUP_W)
            xs_g = xs_s[g]
            b_g = b_s[g]
            c_g = c_s[g]
            cb = lax.dot_general(c_g, b_g.astype(BF16), (((1,), (1,)), ((), ())),
                                 preferred_element_type=F32)
            bt_g = b_g.T
            s_old = state_ref[g]
            y_off = jnp.dot(c_g, s_old.astype(BF16), preferred_element_type=F32)
            esc = (jnp.dot(e_hi, expand_ref[:, gs], preferred_element_type=F32)
                   + jnp.dot(e_lo, expand_ref[:, gs], preferred_element_type=F32))
            xs_b = xs_g.astype(BF16)
            m_cat, bts_cat = [], []
            for r in range(HEADS_PER_GROUP):
                h = g * HEADS_PER_GROUP + r
                seg = acs[:, h:h + 1] - acs_t[h:h + 1, :]
                dec = jnp.where(causal, jnp.exp(seg), 0.0)
                m_cat.append((cb * dec * dt_t[h:h + 1, :]).astype(BF16))
                bts_cat.append((bt_g * w_t[h:h + 1, :]).astype(BF16))
            xs_bd = jnp.concatenate([xs_b * head_mask[r] for r in range(HEADS_PER_GROUP)], axis=0)
            y_diag = jnp.dot(jnp.concatenate(m_cat, axis=1), xs_bd, preferred_element_type=F32)
            ds = jnp.dot(jnp.concatenate(bts_cat, axis=1), xs_bd, preferred_element_type=F32)
            state_ref[g] = s_old * esc[q - 1:q, :] + ds
            y = y_diag + y_off * esc + dskip_ref[:, gs] * xs_g
            y = y * _silu(z_ref[0, rows, gs].astype(F32))
            y_ref[0, rows, gs] = _rms(y, ng_ref[:, gs]).astype(BF16)
        return carry

    lax.fori_loop(0, n_sub, one_chunk, 0)


def _ssd(xbc, z, dt, conv_w, conv_b, dtb, alog, dskip, ng, tri, expand, u, band, pw, pscale):
    b, l, _ = xbc.shape
    rows = SSD_STEP_CHUNKS * CHUNK if l % (SSD_STEP_CHUNKS * CHUNK) == 0 else CHUNK
    cmap = lambda i, c: (0, 0)
    cmap3 = lambda i, c: (0, 0, 0)
    blk = lambda i, c: (i, c, 0)
    return pl.pallas_call(
        _ssd_kernel,
        grid=(b, l // rows),
        in_specs=[
            pl.BlockSpec((1, rows, D_CONV), blk),
            pl.BlockSpec((1, rows, D_SSD), blk),
            pl.BlockSpec((1, rows, HEAD_PAD), blk),
            pl.BlockSpec((CONV_WIDTH, D_CONV), cmap),
            pl.BlockSpec((1, D_CONV), cmap),
            pl.BlockSpec((1, HEAD_PAD), cmap),
            pl.BlockSpec((1, HEAD_PAD), cmap),
            pl.BlockSpec((1, D_SSD), cmap),
            pl.BlockSpec((1, D_SSD), cmap),
            pl.BlockSpec((CHUNK, CHUNK), cmap),
            pl.BlockSpec((HEAD_PAD, D_SSD), cmap),
            pl.BlockSpec((1, rows, D_POOL), blk),
            pl.BlockSpec((len(POOL_WINDOWS), CHUNK, CHUNK + POOL_HIST), cmap3),
            pl.BlockSpec((len(POOL_WINDOWS), POOL_CH, POOL_CH), cmap3),
            pl.BlockSpec((1, D_POOL), cmap),
        ],
        out_specs=[pl.BlockSpec((1, rows, D_SSD), blk), pl.BlockSpec((1, rows, D_POOL), blk)],
        out_shape=[jax.ShapeDtypeStruct((b, l, D_SSD), BF16),
                   jax.ShapeDtypeStruct((b, l, D_POOL), BF16)],
        scratch_shapes=[
            pltpu.VMEM((SSD_GROUPS, D_STATE, GROUP_W), F32),
            pltpu.VMEM((8, D_CONV), F32),
            pltpu.VMEM((SSD_GROUPS, CHUNK, GROUP_W), F32),
            pltpu.VMEM((SSD_GROUPS, CHUNK, D_STATE), F32),
            pltpu.VMEM((SSD_GROUPS, CHUNK, D_STATE), BF16),
            pltpu.VMEM((POOL_HIST + CHUNK, D_POOL), BF16),
        ],
        compiler_params=pltpu.CompilerParams(
            dimension_semantics=("arbitrary", "arbitrary"), vmem_limit_bytes=VMEM_LIMIT),
        name="ssd",
    )(xbc, z, dt, conv_w, conv_b, dtb, alog, dskip, ng, tri, expand, u, band, pw, pscale)


def _outproj_kernel(ys_ref, yp_ref, x_ref, ws_ref, wp_ref, g_ref, rwt_ref, rb_ref, utri_ref,
                    x1_ref, h_ref, idx_ref, gate_ref, rank_ref, cnt_ref, carry_ref):
    @pl.when(pl.program_id(0) == 0)
    def _():
        carry_ref[...] = jnp.zeros_like(carry_ref)

    acc = jnp.dot(ys_ref[...], ws_ref[...], preferred_element_type=F32)
    acc = acc + jnp.dot(yp_ref[...], wp_ref[...], preferred_element_type=F32)
    x1 = x_ref[...] + acc
    x1_ref[...] = x1
    h = _rms(x1, g_ref[...])
    _store_packed_rows(h_ref, h)
    lt = lax.dot_general(rwt_ref[...], h.astype(BF16), (((1,), (1,)), ((), ())),
                         preferred_element_type=F32) + rb_ref[...]
    tr = utri_ref.shape[0]
    for s in range(idx_ref.shape[0]):
        _route_tile(lt[:, s * tr:(s + 1) * tr], utri_ref, idx_ref.at[s], gate_ref.at[s],
                    rank_ref.at[s], carry_ref)
    cnt_ref[...] = carry_ref[...]


def _outproj(y_ssd, y_pool, x2d, w_s, w_p, g, rwt, rb, utri):
    t = x2d.shape[0]
    tm = min(TM_OUT, t)
    tr = utri.shape[0]
    nt = t // tr
    c2 = lambda i: (0, 0)
    o3 = lambda i: (i, 0, 0)
    return pl.pallas_call(
        _outproj_kernel,
        grid=(t // tm,),
        in_specs=[
            pl.BlockSpec((tm, D_SSD), lambda i: (i, 0)),
            pl.BlockSpec((tm, D_POOL), lambda i: (i, 0)),
            pl.BlockSpec((tm, D_MODEL), lambda i: (i, 0)),
            pl.BlockSpec((D_SSD, D_MODEL), c2),
            pl.BlockSpec((D_POOL, D_MODEL), c2),
            pl.BlockSpec((1, D_MODEL), c2),
            pl.BlockSpec((N_EXPERTS, D_MODEL), c2),
            pl.BlockSpec((N_EXPERTS, 1), c2),
            pl.BlockSpec((tr, tr), c2),
        ],
        out_specs=[
            pl.BlockSpec((tm, D_MODEL), lambda i: (i, 0)),
            pl.BlockSpec((tm * ROW_TILE, LANES), lambda i: (i, 0)),
            pl.BlockSpec((tm // tr, TOP_K, tr), o3),
            pl.BlockSpec((tm // tr, TOP_K, tr), o3),
            pl.BlockSpec((tm // tr, TOP_K, tr), o3),
            pl.BlockSpec((N_EXPERTS, LANES), c2),
        ],
        out_shape=[
            jax.ShapeDtypeStruct((t, D_MODEL), F32),
            jax.ShapeDtypeStruct((t * ROW_TILE, LANES), U32),
            jax.ShapeDtypeStruct((nt, TOP_K, tr), I32),
            jax.ShapeDtypeStruct((nt, TOP_K, tr), F32),
            jax.ShapeDtypeStruct((nt, TOP_K, tr), I32),
            jax.ShapeDtypeStruct((N_EXPERTS, LANES), F32),
        ],
        scratch_shapes=[pltpu.VMEM((N_EXPERTS, LANES), F32)],
        compiler_params=pltpu.CompilerParams(
            dimension_semantics=("arbitrary",), vmem_limit_bytes=VMEM_LIMIT),
        name="outproj",
    )(y_ssd, y_pool, x2d, w_s, w_p, g, rwt, rb, utri)


def _route_tile(l, utri_ref, idx_ref, gate_ref, rank_ref, carry_ref):
    tr = l.shape[1]
    eidx = lax.broadcasted_iota(I32, (N_EXPERTS, tr), 0).astype(F32)
    sels, vals, idxs = [], [], []
    for _ in range(TOP_K):
        m = jnp.max(l, axis=0, keepdims=True)
        ik = jnp.min(jnp.where(l == m, eidx, float(N_EXPERTS)), axis=0, keepdims=True)
        sel = eidx == ik
        l = jnp.where(sel, -jnp.inf, l)
        sels.append(sel)
        vals.append(m)
        idxs.append(ik)
    es = [jnp.exp(v - vals[0]) for v in vals]
    den = es[0] + es[1] + es[2] + es[3]
    multi = jnp.zeros((N_EXPERTS, tr), F32)
    for sel in sels:
        multi = multi + jnp.where(sel, 1.0, 0.0)
    incl = jnp.dot(multi.astype(BF16), utri_ref[...], preferred_element_type=F32)
    excl = incl - multi + carry_ref[:, 0:1]
    for k in range(TOP_K):
        idx_ref[k:k + 1, :] = idxs[k].astype(I32)
        gate_ref[k:k + 1, :] = es[k] / den
        rk = jnp.sum(jnp.where(sels[k], excl, 0.0), axis=0, keepdims=True)
        rank_ref[k:k + 1, :] = rk.astype(I32)
    carry_ref[...] = carry_ref[...] + jnp.sum(multi, axis=1, keepdims=True)


def _dest_kernel(pstart_ref, idx_ref, rank_ref, dest_ref):
    idx = idx_ref[...]
    base = jnp.zeros_like(idx)
    for e in range(N_EXPERTS):
        base = jnp.where(idx == e, pstart_ref[e], base)
    dest_ref[...] = base + rank_ref[...]


def _dest(pstart, idx, rank):
    nt, _, tr = idx.shape
    tb = min(DEST_TILES, nt)
    o3 = lambda i, ps: (i, 0, 0)
    return pl.pallas_call(
        _dest_kernel,
        grid_spec=pltpu.PrefetchScalarGridSpec(
            num_scalar_prefetch=1,
            grid=(nt // tb,),
            in_specs=[pl.BlockSpec((tb, TOP_K, tr), o3), pl.BlockSpec((tb, TOP_K, tr), o3)],
            out_specs=pl.BlockSpec((tb, TOP_K, tr), o3),
        ),
        out_shape=jax.ShapeDtypeStruct((nt, TOP_K, tr), I32),
        compiler_params=pltpu.CompilerParams(dimension_semantics=("arbitrary",)),
        name="dest",
    )(pstart, idx, rank)


def _row_tile(r):
    return pl.ds(pl.multiple_of(r * ROW_TILE, ROW_TILE), ROW_TILE)


def _dispatch_kernel(cnt_ref, pad_ref, pstart_ref, nu_ref, dest_hbm, h_ref, xs_hbm, dsm, zrow, sem_d,
                     sem, sem_z):
    i = pl.program_id(0)
    w = dsm.shape[0] // 2
    tr = w // TOP_K
    zrows = zrow.shape[0]
    nb = xs_hbm.shape[0] // zrows
    slot = lax.rem(i, 2)

    def dest_copy(step, s):
        return pltpu.make_async_copy(dest_hbm.at[step], dsm.at[pl.ds(pl.multiple_of(s * w, w), w)],
                                     sem_d)

    def zero_copy(r):
        return pltpu.make_async_copy(zrow.at[pl.ds(0, ROW_TILE), :], xs_hbm.at[_row_tile(r), :], sem)

    def zero_block_copy(blk):
        rows = pl.ds(pl.multiple_of(blk * zrows, zrows), zrows)
        return pltpu.make_async_copy(zrow, xs_hbm.at[rows, :], sem_z)

    @pl.when(i == 0)
    def _():
        dest_copy(0, 0).start()
        zrow[...] = jnp.zeros_like(zrow)

        def zstart(blk, c):
            zero_block_copy(blk).start()
            return c

        def zwait(blk, c):
            zero_block_copy(blk).wait()
            return c

        lax.fori_loop(nu_ref[0], nb, zstart, 0)
        lax.fori_loop(nu_ref[0], nb, zwait, 0)
        for e in range(N_EXPERTS):
            lo, hi, base = cnt_ref[e], pad_ref[e], pstart_ref[e]

            def start(j, c, base=base):
                zero_copy(base + j).start()
                return c

            def wait(j, c, base=base):
                zero_copy(base + j).wait()
                return c

            lax.fori_loop(lo, hi, start, 0)
            lax.fori_loop(lo, hi, wait, 0)

    dest_copy(i, slot).wait()

    @pl.when(i + 1 < pl.num_programs(0))
    def _():
        dest_copy(i + 1, 1 - slot).start()

    def row_copy(t, k):
        d = dsm[slot * w + k * tr + t]
        return pltpu.make_async_copy(h_ref.at[_row_tile(t), :], xs_hbm.at[_row_tile(d), :], sem)

    def start(t, c):
        for k in range(TOP_K):
            row_copy(t, k).start(priority=k % 2)
        return c

    def wait(t, c):
        for k in range(TOP_K):
            row_copy(t, k).wait()
        return c

    lax.fori_loop(0, tr, start, 0, unroll=2)
    lax.fori_loop(0, tr, wait, 0, unroll=8)


def _dispatch(counts, padded, pstart, n_used, dest2d, h, n_slots):
    nt, w = dest2d.shape
    tr = w // TOP_K
    return pl.pallas_call(
        _dispatch_kernel,
        grid_spec=pltpu.PrefetchScalarGridSpec(
            num_scalar_prefetch=4,
            grid=(nt,),
            in_specs=[
                pl.BlockSpec(memory_space=pl.ANY),
                pl.BlockSpec((tr * ROW_TILE, LANES), lambda i, *_: (i, 0)),
            ],
            out_specs=pl.BlockSpec(memory_space=pl.ANY),
            scratch_shapes=[
                pltpu.SMEM((2 * w,), I32),
                pltpu.VMEM((BM_EXPERT * ROW_TILE, LANES), U32),
                pltpu.SemaphoreType.DMA,
                pltpu.SemaphoreType.DMA,
                pltpu.SemaphoreType.DMA,
            ],
        ),
        out_shape=jax.ShapeDtypeStruct((n_slots * ROW_TILE, LANES), U32),
        compiler_params=pltpu.CompilerParams(dimension_semantics=("arbitrary",)),
        name="dispatch",
    )(counts, padded, pstart, n_used, dest2d, h)


def _expert_kernel(be_ref, nu_ref, xs_ref, wgu_ref, bgu_ref, wd_ref, bd_ref, ys_ref, xb_s, wgu_s,
                   wd_s):
    i = pl.program_id(0)
    bm = xb_s.shape[0]

    @pl.when((i == 0) | (be_ref[i] != be_ref[jnp.maximum(i - 1, 0)]))
    def _():
        rc = 128
        for r in range(0, D_MODEL, rc):
            wgu_s[r:r + rc, :] = wgu_ref[0, r:r + rc, :].astype(BF16)
        for r in range(0, D_FF, rc):
            wd_s[r:r + rc, :] = wd_ref[0, r:r + rc, :].astype(BF16)

    @pl.when(i < nu_ref[0])
    def _():
        for c in range(ROW_TILE):
            lo, hi = _load_packed_chunk(xs_ref, c, bm)
            xb_s[:, c * LANES:(c + 1) * LANES] = lo.astype(BF16)
            xb_s[:, HALF + c * LANES:HALF + (c + 1) * LANES] = hi.astype(BF16)
        gu = jnp.dot(xb_s[...], wgu_s[...], preferred_element_type=F32) + bgu_ref[0]
        gt = jnp.minimum(gu[:, :D_FF], SWIGLU_LIMIT)
        up = jnp.clip(gu[:, D_FF:], -SWIGLU_LIMIT, SWIGLU_LIMIT)
        hg = 0.5 * gt
        act = (up + 1.0) * (hg + hg * jnp.tanh(SWIGLU_ALPHA * hg))
        y = jnp.dot(act.astype(BF16), wd_s[...], preferred_element_type=F32) + bd_ref[0]
        _store_packed_rows(ys_ref, y)

    @pl.when(i >= nu_ref[0])
    def _():
        ys_ref[...] = jnp.zeros_like(ys_ref)


def _experts(block_e, n_used, xs, wgu, bgu, wd, bd):
    bm = BM_EXPERT
    nb = xs.shape[0] // (bm * ROW_TILE)
    row = lambda i, be, nu: (jnp.minimum(i, nu[0] - 1), 0)
    orow = lambda i, be, nu: (i, 0)
    wsel = lambda i, be, nu: (be[i], 0, 0)
    return pl.pallas_call(
        _expert_kernel,
        grid_spec=pltpu.PrefetchScalarGridSpec(
            num_scalar_prefetch=2,
            grid=(nb,),
            in_specs=[
                pl.BlockSpec((bm * ROW_TILE, LANES), row),
                pl.BlockSpec((1, D_MODEL, 2 * D_FF), wsel),
                pl.BlockSpec((1, 1, 2 * D_FF), wsel),
                pl.BlockSpec((1, D_FF, D_MODEL), wsel),
                pl.BlockSpec((1, 1, D_MODEL), wsel),
            ],
            out_specs=pl.BlockSpec((bm * ROW_TILE, LANES), orow),
            scratch_shapes=[
                pltpu.VMEM((bm, D_MODEL), BF16),
                pltpu.VMEM((D_MODEL, 2 * D_FF), BF16),
                pltpu.VMEM((D_FF, D_MODEL), BF16),
            ],
        ),
        out_shape=jax.ShapeDtypeStruct(xs.shape, U32),
        compiler_params=pltpu.CompilerParams(
            dimension_semantics=("arbitrary",), vmem_limit_bytes=VMEM_LIMIT),
        name="experts",
    )(block_e, n_used, xs, wgu, bgu, wd, bd)


def _combine_kernel(dest_hbm, ys_hbm, x1_ref, gate_ref, p_ref, gng_ref, wg_ref, wp_ref, png_ref,
                    fng_ref, o_ref, dsm, ybuf, sem_d, sem):
    i = pl.program_id(0)
    tr = x1_ref.shape[0]
    w = TOP_K * tr
    slot = lax.rem(i, 2)

    def dest_copy(step, s):
        return pltpu.make_async_copy(dest_hbm.at[step], dsm.at[pl.ds(pl.multiple_of(s * w, w), w)],
                                     sem_d)

    def row_copy(s, t, k):
        d = dsm[s * w + k * tr + t]
        return pltpu.make_async_copy(ys_hbm.at[_row_tile(d), :], ybuf.at[s, k, _row_tile(t), :],
                                     sem.at[s])

    def gather_tile(step, s):
        cp = dest_copy(step, s)
        cp.start()
        cp.wait()

        def start(t, c):
            for k in range(TOP_K):
                row_copy(s, t, k).start(priority=k % 2)
            return c

        lax.fori_loop(0, tr, start, 0, unroll=2)

    @pl.when(i == 0)
    def _():
        gather_tile(0, 0)

    @pl.when(i + 1 < pl.num_programs(0))
    def _():
        gather_tile(i + 1, 1 - slot)

    def wait(t, c):
        for k in range(TOP_K):
            row_copy(slot, t, k).wait()
        return c

    lax.fori_loop(0, tr, wait, 0, unroll=8)

    lo_chunks, hi_chunks = [], []
    for c in range(ROW_TILE):
        acc_lo = x1_ref[:, c * LANES:(c + 1) * LANES]
        acc_hi = x1_ref[:, HALF + c * LANES:HALF + (c + 1) * LANES]
        for k in range(TOP_K):
            lo, hi = _load_packed_chunk(ybuf.at[slot, k], c, tr)
            g = gate_ref[:, k:k + 1]
            acc_lo = acc_lo + g * lo
            acc_hi = acc_hi + g * hi
        lo_chunks.append(acc_lo)
        hi_chunks.append(acc_hi)
    x2 = jnp.concatenate(lo_chunks + hi_chunks, axis=1)
    gate = jnp.dot(_rms(x2, gng_ref[...]).astype(BF16), wg_ref[...], preferred_element_type=F32)
    gate = 1.0 / (1.0 + jnp.exp(-gate))
    e = jnp.dot(p_ref[...].astype(BF16), wp_ref[...], preferred_element_type=F32)
    x3 = x2 + _rms(e, png_ref[...]) * gate
    o_ref[...] = _rms(x3, fng_ref[...])


def _combine(dest2d, ys, x1, gate_t, p2d, gng, wg, wp, png, fng):
    t = x1.shape[0]
    nt, w = dest2d.shape
    tr = w // TOP_K
    c2 = lambda i: (0, 0)
    return pl.pallas_call(
        _combine_kernel,
        grid=(nt,),
        in_specs=[
            pl.BlockSpec(memory_space=pl.ANY),
            pl.BlockSpec(memory_space=pl.ANY),
            pl.BlockSpec((tr, D_MODEL), lambda i: (i, 0)),
            pl.BlockSpec((tr, TOP_K), lambda i: (i, 0)),
            pl.BlockSpec((tr, PLE_DIM), lambda i: (i, 0)),
            pl.BlockSpec((1, D_MODEL), c2),
            pl.BlockSpec((D_MODEL, D_MODEL), c2),
            pl.BlockSpec((PLE_DIM, D_MODEL), c2),
            pl.BlockSpec((1, D_MODEL), c2),
            pl.BlockSpec((1, D_MODEL), c2),
        ],
        out_specs=pl.BlockSpec((tr, D_MODEL), lambda i: (i, 0)),
        out_shape=jax.ShapeDtypeStruct((t, D_MODEL), F32),
        scratch_shapes=[
            pltpu.SMEM((2 * w,), I32),
            pltpu.VMEM((2, TOP_K, tr * ROW_TILE, LANES), U32),
            pltpu.SemaphoreType.DMA,
            pltpu.SemaphoreType.DMA((2,)),
        ],
        compiler_params=pltpu.CompilerParams(
            dimension_semantics=("arbitrary",), vmem_limit_bytes=VMEM_LIMIT),
        name="combine",
    )(dest2d, ys, x1, gate_t, p2d, gng, wg, wp, png, fng)


def _pad_heads(v):
    return jnp.pad(v.astype(F32), (0, HEAD_PAD - SSD_HEADS)).reshape(1, HEAD_PAD)


def _layer(x, p, mix_norm_g, w_in, conv_w, conv_b, dt_bias, a_log, d_skip, ssd_norm_g, pool_w,
           pool_scale, w_out, ffn_norm_g, router_w, router_b, w_gate_up, b_gate_up, w_down,
           b_down, ple_gate_norm_g, w_ple_gate, w_ple_proj, ple_norm_g, final_g):
    b, l, _ = x.shape
    t = b * l
    x2d = x.reshape(t, D_MODEL)
    row = lambda v: v.reshape(1, -1).astype(F32)

    c_dt = D_SSD + D_CONV
    c_pool = c_dt + SSD_HEADS
    w_main = w_in.astype(BF16)
    w_tail = jnp.concatenate(
        [w_in[:, c_pool:], w_in[:, c_dt:c_pool],
         jnp.zeros((D_MODEL, HEAD_PAD - SSD_HEADS), w_in.dtype)], axis=1).astype(BF16)
    ii = jnp.arange(CHUNK)
    tri = (ii[None, :] <= ii[:, None]).astype(BF16)
    rr = jnp.arange(CHUNK)[:, None] + POOL_HIST
    jj = jnp.arange(CHUNK + POOL_HIST)[None, :]
    band = jnp.stack([((jj <= rr) & (jj > rr - w)) for w in POOL_WINDOWS]).astype(BF16)
    tr = min(TR_ROUTE, t)
    ri = jnp.arange(tr)
    utri = (ri[:, None] <= ri[None, :]).astype(BF16)

    head_of_col = jnp.arange(D_SSD) // SSD_HEAD_DIM
    expand = (jnp.arange(HEAD_PAD)[:, None] == head_of_col[None, :]).astype(BF16)

    z, xbc, pool_in, dt = _inproj(x2d, row(mix_norm_g), w_main, w_tail)
    y_ssd, y_pool = _ssd(xbc.reshape(b, l, D_CONV), z.reshape(b, l, D_SSD),
                         dt.reshape(b, l, HEAD_PAD), conv_w.astype(F32), row(conv_b),
                         _pad_heads(dt_bias), _pad_heads(a_log),
                         row(jnp.repeat(d_skip, SSD_HEAD_DIM)), row(ssd_norm_g), tri, expand,
                         pool_in.reshape(b, l, D_POOL), band, pool_w.astype(BF16), row(pool_scale))
    x1, h, idx, gate, rank, cnt = _outproj(
        y_ssd.reshape(t, D_SSD), y_pool.reshape(t, D_POOL), x2d, w_out[:D_SSD].astype(BF16),
        w_out[D_SSD:].astype(BF16), row(ffn_norm_g), router_w.T.astype(BF16),
        router_b.reshape(N_EXPERTS, 1).astype(F32), utri)
    counts = cnt[:, 0].astype(I32)
    bm = BM_EXPERT
    padded = ((counts + bm - 1) // bm) * bm
    pend = jnp.cumsum(padded)
    pstart = (pend - padded).astype(I32)
    nb = (t * TOP_K) // bm + N_EXPERTS
    n_used = (pend[-1] // bm).astype(I32)
    blk = jnp.arange(nb, dtype=I32)
    block_e = jnp.sum((pend[None, :] <= (blk * bm)[:, None]).astype(I32), axis=1)
    block_e = jnp.minimum(block_e, N_EXPERTS - 1)
    last_e = jnp.sum((pend <= (n_used - 1) * bm).astype(I32))
    block_e = jnp.where(blk < n_used, block_e, jnp.minimum(last_e, N_EXPERTS - 1)).astype(I32)

    dest = _dest(pstart, idx, rank)
    nt = t // tr
    dest2d = dest.reshape(nt, TOP_K * tr)
    xs = _dispatch(counts, padded.astype(I32), pstart, n_used.reshape(1), dest2d, h, nb * bm)
    ys = _experts(block_e, n_used.reshape(1), xs, w_gate_up.astype(F32),
                  b_gate_up.reshape(N_EXPERTS, 1, 2 * D_FF).astype(F32), w_down.astype(F32),
                  b_down.reshape(N_EXPERTS, 1, D_MODEL).astype(F32))
    gate_t = gate.transpose(0, 2, 1).reshape(t, TOP_K)
    out = _combine(dest2d, ys, x1, gate_t, p.reshape(t, PLE_DIM), row(ple_gate_norm_g),
                   w_ple_gate.astype(BF16), w_ple_proj.astype(BF16), row(ple_norm_g), row(final_g))
    return out.reshape(b, l, D_MODEL)


def kernel(x, p, mix_norm_g, w_in, conv_w, conv_b, dt_bias, a_log, d_skip, ssd_norm_g, pool_w, pool_scale, w_out, ffn_norm_g, router_w, router_b, w_gate_up, b_gate_up, w_down, b_down, ple_gate_norm_g, w_ple_gate, w_ple_proj, ple_norm_g, final_norm_g):
    assert x.shape[-1] == D_MODEL and mix_norm_g.shape[0] == 1
    layer0 = lambda v: v.reshape(v.shape[1:])
    per_layer = (p, mix_norm_g, w_in, conv_w, conv_b, dt_bias, a_log, d_skip, ssd_norm_g, pool_w,
                 pool_scale, w_out, ffn_norm_g, router_w, router_b, w_gate_up, b_gate_up, w_down,
                 b_down, ple_gate_norm_g, w_ple_gate, w_ple_proj, ple_norm_g)
    return _layer(x, *[layer0(v) for v in per_layer], final_norm_g)
```

```python
import jax
import jax.numpy as jnp
from jax import lax
from jax.experimental import pallas as pl
from jax.experimental.pallas import tpu as pltpu

F32 = jnp.float32
BF16 = jnp.bfloat16
I32 = jnp.int32

D_MODEL = 1024
SSD_HEADS = 32
SSD_HEAD_DIM = 64
D_SSD = SSD_HEADS * SSD_HEAD_DIM
SSD_GROUPS = 8
HEADS_PER_GROUP = SSD_HEADS // SSD_GROUPS
GROUP_W = HEADS_PER_GROUP * SSD_HEAD_DIM
D_STATE = 128
CONV_WIDTH = 4
CHUNK = 128
D_BC = SSD_GROUPS * D_STATE
D_CONV = D_SSD + 2 * D_BC
POOL_WINDOWS = (2, 4, 8, 16)
POOL_CH = 256
D_POOL = len(POOL_WINDOWS) * POOL_CH
POOL_HIST = 128
N_EXPERTS = 32
TOP_K = 4
D_FF = 1024
SWIGLU_LIMIT = 7.0
SWIGLU_ALPHA = 1.702
PLE_DIM = 256
EPS = 1e-6

LANES = 128
HEAD_PAD = LANES

TM_IN = 512
SSD_STEP_CHUNKS = 4
TM_OUT = 512
TR_ROUTE = 256
DEST_TILES = 16
BM_EXPERT = 512
VMEM_LIMIT = 56 * 1024 * 1024


def _rms(x, g):
    return x * lax.rsqrt(jnp.mean(x * x, axis=-1, keepdims=True) + EPS) * g


def _silu(x):
    h = 0.5 * x
    return h + h * jnp.tanh(h)


HALF = D_MODEL // 2
ROW_TILE = HALF // LANES
U32 = jnp.uint32


def _store_packed_rows(ref, v):
    m = v.shape[0]
    u = lax.bitcast_convert_type(v, U32)
    r = u + jnp.uint32(0x7FFF) + ((u >> 16) & jnp.uint32(1))
    w = (r[:, :HALF] >> 16) | (r[:, HALF:] & jnp.uint32(0xFFFF0000))
    for c in range(ROW_TILE):
        ref[pl.ds(c, m, stride=ROW_TILE), :] = w[:, c * LANES:(c + 1) * LANES]


def _load_packed_chunk(ref, c, m):
    w = ref[pl.ds(c, m, stride=ROW_TILE), :]
    lo = lax.bitcast_convert_type(w << 16, F32)
    hi = lax.bitcast_convert_type(w & jnp.uint32(0xFFFF0000), F32)
    return lo, hi


def _split3(v):
    hi = v.astype(BF16)
    r1 = v - hi.astype(F32)
    mid = r1.astype(BF16)
    lo = (r1 - mid.astype(F32)).astype(BF16)
    return hi, mid, lo


D_IN = D_SSD + D_CONV + SSD_HEADS + D_POOL
W_TAIL_COLS = D_POOL + HEAD_PAD


def _inproj_kernel(x_ref, g_ref, wm_ref, wt_ref, z_ref, xbc_ref, pool_ref, dt_ref):
    hb = _rms(x_ref[...], g_ref[...]).astype(BF16)

    def mm(w_ref, c0, n):
        return jnp.dot(hb, w_ref[:, c0:c0 + n], preferred_element_type=F32)

    cw = 512
    for c in range(0, D_SSD, cw):
        z_ref[:, c:c + cw] = mm(wm_ref, c, cw).astype(BF16)
    for c in range(0, D_CONV, cw):
        xbc_ref[:, c:c + cw] = mm(wm_ref, D_SSD + c, cw).astype(BF16)
    for c in range(0, D_POOL, cw):
        pool_ref[:, c:c + cw] = mm(wt_ref, c, cw).astype(BF16)
    dt_ref[...] = mm(wt_ref, D_POOL, HEAD_PAD)


def _inproj(x2d, g, w_main, w_tail):
    t = x2d.shape[0]
    tm = min(TM_IN, t)
    return pl.pallas_call(
        _inproj_kernel,
        grid=(t // tm,),
        in_specs=[
            pl.BlockSpec((tm, D_MODEL), lambda i: (i, 0)),
            pl.BlockSpec((1, D_MODEL), lambda i: (0, 0)),
            pl.BlockSpec((D_MODEL, D_IN), lambda i: (0, 0), pipeline_mode=pl.Buffered(1)),
            pl.BlockSpec((D_MODEL, W_TAIL_COLS), lambda i: (0, 0), pipeline_mode=pl.Buffered(1)),
        ],
        out_specs=[
            pl.BlockSpec((tm, D_SSD), lambda i: (i, 0)),
            pl.BlockSpec((tm, D_CONV), lambda i: (i, 0)),
            pl.BlockSpec((tm, D_POOL), lambda i: (i, 0)),
            pl.BlockSpec((tm, HEAD_PAD), lambda i: (i, 0)),
        ],
        out_shape=[
            jax.ShapeDtypeStruct((t, D_SSD), BF16),
            jax.ShapeDtypeStruct((t, D_CONV), BF16),
            jax.ShapeDtypeStruct((t, D_POOL), BF16),
            jax.ShapeDtypeStruct((t, HEAD_PAD), F32),
        ],
        compiler_params=pltpu.CompilerParams(
            dimension_semantics=("arbitrary",), vmem_limit_bytes=VMEM_LIMIT),
        name="inproj",
    )(x2d, g, w_main, w_tail)


def _ssd_kernel(xbc_ref, z_ref, dt_ref, convw_ref, convb_ref, dtb_ref, alog_ref, dskip_ref,
                ng_ref, tri_ref, expand_ref, u_ref, band_ref, pw_ref, pscale_ref, y_ref, yp_ref,
                state_ref, hist_ref, xs_s, b_s, c_s, phist_ref):
    q = CHUNK
    n_sub = xbc_ref.shape[1] // q

    @pl.when(pl.program_id(1) == 0)
    def _():
        state_ref[...] = jnp.zeros_like(state_ref)
        hist_ref[...] = jnp.zeros_like(hist_ref)
        phist_ref[0:POOL_HIST, :] = jnp.zeros((POOL_HIST, D_POOL), BF16)

    def one_chunk(sub, carry):
        rows = pl.ds(pl.multiple_of(sub * q, q), q)
        li = pl.program_id(1) * n_sub + sub

        phist_ref[POOL_HIST:POOL_HIST + q, :] = u_ref[0, rows, :]
        pos = li * q + lax.broadcasted_iota(I32, (q, POOL_CH), 0)
        for gi, w in enumerate(POOL_WINDOWS):
            ps = slice(gi * POOL_CH, (gi + 1) * POOL_CH)
            win_sum = jnp.dot(band_ref[gi], phist_ref[:, ps], preferred_element_type=F32)
            cnt = jnp.minimum(pos + 1, w).astype(F32)
            pre = (win_sum / cnt - u_ref[0, rows, ps].astype(F32)).astype(BF16)
            yp = jnp.dot(pre, pw_ref[gi], preferred_element_type=F32) * pscale_ref[:, ps]
            yp_ref[0, rows, ps] = yp.astype(BF16)
        phist_ref[0:POOL_HIST, :] = phist_ref[q:q + POOL_HIST, :]

        cw = GROUP_W
        hr = hist_ref.shape[0]
        for ci in range(D_CONV // cw):
            cs = slice(ci * cw, (ci + 1) * cw)
            cur = xbc_ref[0, rows, cs].astype(F32)
            xe = jnp.concatenate([hist_ref[:, cs], cur], axis=0)
            hist_ref[:, cs] = cur[q - hr:q, :]
            s1 = pltpu.roll(xe, 1, 0)
            u = convw_ref[3:4, cs] * xe + convw_ref[2:3, cs] * s1
            w = convw_ref[1:2, cs] * xe + convw_ref[0:1, cs] * s1
            acc = convb_ref[:, cs] + u + pltpu.roll(w, 2, 0)
            v = _silu(acc[hr:, :])
            if ci < D_SSD // cw:
                xs_s[ci] = v
            elif ci < (D_SSD + D_BC) // cw:
                j = ci - D_SSD // cw
                b_s[2 * j] = v[:, :D_STATE]
                b_s[2 * j + 1] = v[:, D_STATE:]
            else:
                j = ci - (D_SSD + D_BC) // cw
                c_s[2 * j] = v[:, :D_STATE].astype(BF16)
                c_s[2 * j + 1] = v[:, D_STATE:].astype(BF16)

        xdt = dt_ref[0, rows, :] + dtb_ref[...]
        dt = jnp.maximum(xdt, 0.0) + jnp.log1p(jnp.exp(-jnp.abs(xdt)))
        a = dt * (-jnp.exp(alog_ref[...]))
        tri = tri_ref[...]
        a_hi, a_mid, a_lo = _split3(a)
        acs = (jnp.dot(tri, a_hi, preferred_element_type=F32)
               + jnp.dot(tri, a_mid, preferred_element_type=F32)
               + jnp.dot(tri, a_lo, preferred_element_type=F32))
        acs_t = acs.T
        dt_t = dt.T
        eacs = jnp.exp(acs)
        w_t = jnp.exp(acs_t[:, q - 1:q] - acs_t) * dt_t
        e_hi = eacs.astype(BF16)
        e_lo = (eacs - e_hi.astype(F32)).astype(BF16)

        row = lax.broadcasted_iota(I32, (q, q), 0)
        col = lax.broadcasted_iota(I32, (q, q), 1)
        causal = col <= row
        head_shift = SSD_HEAD_DIM.bit_length() - 1
        lane_head = jnp.right_shift(lax.broadcasted_iota(I32, (q, GROUP_W), 1), head_shift)
        head_mask = [jnp.where(lane_head == r, 1.0, 0.0).astype(BF16)
                     for r in range(HEADS_PER_GROUP)]

        for g in range(SSD_GROUPS):
            gs = slice(g * GROUP_W, (g + 1) * GROUP_W)
            xs_g = xs_s[g]
            b_g = b_s[g]
            c_g = c_s[g]
            cb = lax.dot_general(c_g, b_g.astype(BF16), (((1,), (1,)), ((), ())),
                                 preferred_element_type=F32)
            bt_g = b_g.T
            s_old = state_ref[g]
            y_off = jnp.dot(c_g, s_old.astype(BF16), preferred_element_type=F32)
            esc = (jnp.dot(e_hi, expand_ref[:, gs], preferred_element_type=F32)
                   + jnp.dot(e_lo, expand_ref[:, gs], preferred_element_type=F32))
            xs_b = xs_g.astype(BF16)
            m_cat, bts_cat = [], []
            for r in range(HEADS_PER_GROUP):
                h = g * HEADS_PER_GROUP + r
                seg = acs[:, h:h + 1] - acs_t[h:h + 1, :]
                dec = jnp.where(causal, jnp.exp(seg), 0.0)
                m_cat.append((cb * dec * dt_t[h:h + 1, :]).astype(BF16))
                bts_cat.append((bt_g * w_t[h:h + 1, :]).astype(BF16))
            xs_bd = jnp.concatenate([xs_b * head_mask[r] for r in range(HEADS_PER_GROUP)], axis=0)
            y_diag = jnp.dot(jnp.concatenate(m_cat, axis=1), xs_bd, preferred_element_type=F32)
            ds = jnp.dot(jnp.concatenate(bts_cat, axis=1), xs_bd, preferred_element_type=F32)
            state_ref[g] = s_old * esc[q - 1:q, :] + ds
            y = y_diag + y_off * esc + dskip_ref[:, gs] * xs_g
            y = y * _silu(z_ref[0, rows, gs].astype(F32))
            y_ref[0, rows, gs] = _rms(y, ng_ref[:, gs]).astype(BF16)
        return carry

    lax.fori_loop(0, n_sub, one_chunk, 0)


def _ssd(xbc, z, dt, conv_w, conv_b, dtb, alog, dskip, ng, tri, expand, u, band, pw, pscale):
    b, l, _ = xbc.shape
    rows = SSD_STEP_CHUNKS * CHUNK if l % (SSD_STEP_CHUNKS * CHUNK) == 0 else CHUNK
    cmap = lambda i, c: (0, 0)
    cmap3 = lambda i, c: (0, 0, 0)
    blk = lambda i, c: (i, c, 0)
    return pl.pallas_call(
        _ssd_kernel,
        grid=(b, l // rows),
        in_specs=[
            pl.BlockSpec((1, rows, D_CONV), blk),
            pl.BlockSpec((1, rows, D_SSD), blk),
            pl.BlockSpec((1, rows, HEAD_PAD), blk),
            pl.BlockSpec((CONV_WIDTH, D_CONV), cmap),
            pl.BlockSpec((1, D_CONV), cmap),
            pl.BlockSpec((1, HEAD_PAD), cmap),
            pl.BlockSpec((1, HEAD_PAD), cmap),
            pl.BlockSpec((1, D_SSD), cmap),
            pl.BlockSpec((1, D_SSD), cmap),
            pl.BlockSpec((CHUNK, CHUNK), cmap),
            pl.BlockSpec((HEAD_PAD, D_SSD), cmap),
            pl.BlockSpec((1, rows, D_POOL), blk),
            pl.BlockSpec((len(POOL_WINDOWS), CHUNK, CHUNK + POOL_HIST), cmap3),
            pl.BlockSpec((len(POOL_WINDOWS), POOL_CH, POOL_CH), cmap3),
            pl.BlockSpec((1, D_POOL), cmap),
        ],
        out_specs=[pl.BlockSpec((1, rows, D_SSD), blk), pl.BlockSpec((1, rows, D_POOL), blk)],
        out_shape=[jax.ShapeDtypeStruct((b, l, D_SSD), BF16),
                   jax.ShapeDtypeStruct((b, l, D_POOL), BF16)],
        scratch_shapes=[
            pltpu.VMEM((SSD_GROUPS, D_STATE, GROUP_W), F32),
            pltpu.VMEM((8, D_CONV), F32),
            pltpu.VMEM((SSD_GROUPS, CHUNK, GROUP_W), F32),
            pltpu.VMEM((SSD_GROUPS, CHUNK, D_STATE), F32),
            pltpu.VMEM((SSD_GROUPS, CHUNK, D_STATE), BF16),
            pltpu.VMEM((POOL_HIST + CHUNK, D_POOL), BF16),
        ],
        compiler_params=pltpu.CompilerParams(
            dimension_semantics=("arbitrary", "arbitrary"), vmem_limit_bytes=VMEM_LIMIT),
        name="ssd",
    )(xbc, z, dt, conv_w, conv_b, dtb, alog, dskip, ng, tri, expand, u, band, pw, pscale)


def _outproj_kernel(ys_ref, yp_ref, x_ref, ws_ref, wp_ref, g_ref, rwt_ref, rb_ref, utri_ref,
                    x1_ref, h_ref, idx_ref, gate_ref, rank_ref, cnt_ref, carry_ref):
    @pl.when(pl.program_id(0) == 0)
    def _():
        carry_ref[...] = jnp.zeros_like(carry_ref)

    acc = jnp.dot(ys_ref[...], ws_ref[...], preferred_element_type=F32)
    acc = acc + jnp.dot(yp_ref[...], wp_ref[...], preferred_element_type=F32)
    x1 = x_ref[...] + acc
    x1_ref[...] = x1
    h = _rms(x1, g_ref[...])
    _store_packed_rows(h_ref, h)
    lt = lax.dot_general(rwt_ref[...], h.astype(BF16), (((1,), (1,)), ((), ())),
                         preferred_element_type=F32) + rb_ref[...]
    tr = utri_ref.shape[0]
    for s in range(idx_ref.shape[0]):
        _route_tile(lt[:, s * tr:(s + 1) * tr], utri_ref, idx_ref.at[s], gate_ref.at[s],
                    rank_ref.at[s], carry_ref)
    cnt_ref[...] = carry_ref[...]


def _outproj(y_ssd, y_pool, x2d, w_s, w_p, g, rwt, rb, utri):
    t = x2d.shape[0]
    tm = min(TM_OUT, t)
    tr = utri.shape[0]
    nt = t // tr
    c2 = lambda i: (0, 0)
    o3 = lambda i: (i, 0, 0)
    return pl.pallas_call(
        _outproj_kernel,
        grid=(t // tm,),
        in_specs=[
            pl.BlockSpec((tm, D_SSD), lambda i: (i, 0)),
            pl.BlockSpec((tm, D_POOL), lambda i: (i, 0)),
            pl.BlockSpec((tm, D_MODEL), lambda i: (i, 0)),
            pl.BlockSpec((D_SSD, D_MODEL), c2),
            pl.BlockSpec((D_POOL, D_MODEL), c2),
            pl.BlockSpec((1, D_MODEL), c2),
            pl.BlockSpec((N_EXPERTS, D_MODEL), c2),
            pl.BlockSpec((N_EXPERTS, 1), c2),
            pl.BlockSpec((tr, tr), c2),
        ],
        out_specs=[
            pl.BlockSpec((tm, D_MODEL), lambda i: (i, 0)),
            pl.BlockSpec((tm * ROW_TILE, LANES), lambda i: (i, 0)),
            pl.BlockSpec((tm // tr, TOP_K, tr), o3),
            pl.BlockSpec((tm // tr, TOP_K, tr), o3),
            pl.BlockSpec((tm // tr, TOP_K, tr), o3),
            pl.BlockSpec((N_EXPERTS, LANES), c2),
        ],
        out_shape=[
            jax.ShapeDtypeStruct((t, D_MODEL), F32),
            jax.ShapeDtypeStruct((t * ROW_TILE, LANES), U32),
            jax.ShapeDtypeStruct((nt, TOP_K, tr), I32),
            jax.ShapeDtypeStruct((nt, TOP_K, tr), F32),
            jax.ShapeDtypeStruct((nt, TOP_K, tr), I32),
            jax.ShapeDtypeStruct((N_EXPERTS, LANES), F32),
        ],
        scratch_shapes=[pltpu.VMEM((N_EXPERTS, LANES), F32)],
        compiler_params=pltpu.CompilerParams(
            dimension_semantics=("arbitrary",), vmem_limit_bytes=VMEM_LIMIT),
        name="outproj",
    )(y_ssd, y_pool, x2d, w_s, w_p, g, rwt, rb, utri)


def _route_tile(l, utri_ref, idx_ref, gate_ref, rank_ref, carry_ref):
    tr = l.shape[1]
    eidx = lax.broadcasted_iota(I32, (N_EXPERTS, tr), 0).astype(F32)
    sels, vals, idxs = [], [], []
    for _ in range(TOP_K):
        m = jnp.max(l, axis=0, keepdims=True)
        ik = jnp.min(jnp.where(l == m, eidx, float(N_EXPERTS)), axis=0, keepdims=True)
        sel = eidx == ik
        l = jnp.where(sel, -jnp.inf, l)
        sels.append(sel)
        vals.append(m)
        idxs.append(ik)
    es = [jnp.exp(v - vals[0]) for v in vals]
    den = es[0] + es[1] + es[2] + es[3]
    multi = jnp.zeros((N_EXPERTS, tr), F32)
    for sel in sels:
        multi = multi + jnp.where(sel, 1.0, 0.0)
    incl = jnp.dot(multi.astype(BF16), utri_ref[...], preferred_element_type=F32)
    excl = incl - multi + carry_ref[:, 0:1]
    for k in range(TOP_K):
        idx_ref[k:k + 1, :] = idxs[k].astype(I32)
        gate_ref[k:k + 1, :] = es[k] / den
        rk = jnp.sum(jnp.where(sels[k], excl, 0.0), axis=0, keepdims=True)
        rank_ref[k:k + 1, :] = rk.astype(I32)
    carry_ref[...] = carry_ref[...] + jnp.sum(multi, axis=1, keepdims=True)


def _dest_kernel(pstart_ref, idx_ref, rank_ref, dest_ref):
    idx = idx_ref[...]
    base = jnp.zeros_like(idx)
    for e in range(N_EXPERTS):
        base = jnp.where(idx == e, pstart_ref[e], base)
    dest_ref[...] = base + rank_ref[...]


def _dest(pstart, idx, rank):
    nt, _, tr = idx.shape
    tb = min(DEST_TILES, nt)
    o3 = lambda i, ps: (i, 0, 0)
    return pl.pallas_call(
        _dest_kernel,
        grid_spec=pltpu.PrefetchScalarGridSpec(
            num_scalar_prefetch=1,
            grid=(nt // tb,),
            in_specs=[pl.BlockSpec((tb, TOP_K, tr), o3), pl.BlockSpec((tb, TOP_K, tr), o3)],
            out_specs=pl.BlockSpec((tb, TOP_K, tr), o3),
        ),
        out_shape=jax.ShapeDtypeStruct((nt, TOP_K, tr), I32),
        compiler_params=pltpu.CompilerParams(dimension_semantics=("arbitrary",)),
        name="dest",
    )(pstart, idx, rank)


def _row_tile(r):
    return pl.ds(pl.multiple_of(r * ROW_TILE, ROW_TILE), ROW_TILE)


def _dispatch_kernel(cnt_ref, pad_ref, pstart_ref, nu_ref, dest_hbm, h_ref, xs_hbm, dsm, zrow, sem_d,
                     sem, sem_z):
    i = pl.program_id(0)
    w = dsm.shape[0] // 2
    tr = w // TOP_K
    zrows = zrow.shape[0]
    nb = xs_hbm.shape[0] // zrows
    slot = lax.rem(i, 2)

    def dest_copy(step, s):
        return pltpu.make_async_copy(dest_hbm.at[step], dsm.at[pl.ds(pl.multiple_of(s * w, w), w)],
                                     sem_d)

    def zero_copy(r):
        return pltpu.make_async_copy(zrow.at[pl.ds(0, ROW_TILE), :], xs_hbm.at[_row_tile(r), :], sem)

    def zero_block_copy(blk):
        rows = pl.ds(pl.multiple_of(blk * zrows, zrows), zrows)
        return pltpu.make_async_copy(zrow, xs_hbm.at[rows, :], sem_z)

    @pl.when(i == 0)
    def _():
        dest_copy(0, 0).start()
        zrow[...] = jnp.zeros_like(zrow)

        def zstart(blk, c):
            zero_block_copy(blk).start()
            return c

        def zwait(blk, c):
            zero_block_copy(blk).wait()
            return c

        lax.fori_loop(nu_ref[0], nb, zstart, 0)
        lax.fori_loop(nu_ref[0], nb, zwait, 0)
        for e in range(N_EXPERTS):
            lo, hi, base = cnt_ref[e], pad_ref[e], pstart_ref[e]

            def start(j, c, base=base):
                zero_copy(base + j).start()
                return c

            def wait(j, c, base=base):
                zero_copy(base + j).wait()
                return c

            lax.fori_loop(lo, hi, start, 0)
            lax.fori_loop(lo, hi, wait, 0)

    dest_copy(i, slot).wait()

    @pl.when(i + 1 < pl.num_programs(0))
    def _():
        dest_copy(i + 1, 1 - slot).start()

    def row_copy(t, k):
        d = dsm[slot * w + k * tr + t]
        return pltpu.make_async_copy(h_ref.at[_row_tile(t), :], xs_hbm.at[_row_tile(d), :], sem)

    def start(t, c):
        for k in range(TOP_K):
            row_copy(t, k).start(priority=k % 2)
        return c

    def wait(t, c):
        for k in range(TOP_K):
            row_copy(t, k).wait()
        return c

    lax.fori_loop(0, tr, start, 0, unroll=2)
    lax.fori_loop(0, tr, wait, 0, unroll=8)


def _dispatch(counts, padded, pstart, n_used, dest2d, h, n_slots):
    nt, w = dest2d.shape
    tr = w // TOP_K
    return pl.pallas_call(
        _dispatch_kernel,
        grid_spec=pltpu.PrefetchScalarGridSpec(
            num_scalar_prefetch=4,
            grid=(nt,),
            in_specs=[
                pl.BlockSpec(memory_space=pl.ANY),
                pl.BlockSpec((tr * ROW_TILE, LANES), lambda i, *_: (i, 0)),
            ],
            out_specs=pl.BlockSpec(memory_space=pl.ANY),
            scratch_shapes=[
                pltpu.SMEM((2 * w,), I32),
                pltpu.VMEM((BM_EXPERT * ROW_TILE, LANES), U32),
                pltpu.SemaphoreType.DMA,
                pltpu.SemaphoreType.DMA,
                pltpu.SemaphoreType.DMA,
            ],
        ),
        out_shape=jax.ShapeDtypeStruct((n_slots * ROW_TILE, LANES), U32),
        compiler_params=pltpu.CompilerParams(dimension_semantics=("arbitrary",)),
        name="dispatch",
    )(counts, padded, pstart, n_used, dest2d, h)


def _expert_kernel(be_ref, nu_ref, xs_ref, wgu_ref, bgu_ref, wd_ref, bd_ref, ys_ref, xb_s, wgu_s,
                   wd_s):
    i = pl.program_id(0)
    bm = xb_s.shape[0]

    @pl.when((i == 0) | (be_ref[i] != be_ref[jnp.maximum(i - 1, 0)]))
    def _():
        rc = 128
        for r in range(0, D_MODEL, rc):
            wgu_s[r:r + rc, :] = wgu_ref[0, r:r + rc, :].astype(BF16)
        for r in range(0, D_FF, rc):
            wd_s[r:r + rc, :] = wd_ref[0, r:r + rc, :].astype(BF16)

    @pl.when(i < nu_ref[0])
    def _():
        for c in range(ROW_TILE):
            lo, hi = _load_packed_chunk(xs_ref, c, bm)
            xb_s[:, c * LANES:(c + 1) * LANES] = lo.astype(BF16)
            xb_s[:, HALF + c * LANES:HALF + (c + 1) * LANES] = hi.astype(BF16)
        gu = jnp.dot(xb_s[...], wgu_s[...], preferred_element_type=F32) + bgu_ref[0]
        gt = jnp.minimum(gu[:, :D_FF], SWIGLU_LIMIT)
        up = jnp.clip(gu[:, D_FF:], -SWIGLU_LIMIT, SWIGLU_LIMIT)
        hg = 0.5 * gt
        act = (up + 1.0) * (hg + hg * jnp.tanh(SWIGLU_ALPHA * hg))
        y = jnp.dot(act.astype(BF16), wd_s[...], preferred_element_type=F32) + bd_ref[0]
        _store_packed_rows(ys_ref, y)

    @pl.when(i >= nu_ref[0])
    def _():
        ys_ref[...] = jnp.zeros_like(ys_ref)


def _experts(block_e, n_used, xs, wgu, bgu, wd, bd):
    bm = BM_EXPERT
    nb = xs.shape[0] // (bm * ROW_TILE)
    row = lambda i, be, nu: (jnp.minimum(i, nu[0] - 1), 0)
    orow = lambda i, be, nu: (i, 0)
    wsel = lambda i, be, nu: (be[i], 0, 0)
    return pl.pallas_call(
        _expert_kernel,
        grid_spec=pltpu.PrefetchScalarGridSpec(
            num_scalar_prefetch=2,
            grid=(nb,),
            in_specs=[
                pl.BlockSpec((bm * ROW_TILE, LANES), row),
                pl.BlockSpec((1, D_MODEL, 2 * D_FF), wsel),
                pl.BlockSpec((1, 1, 2 * D_FF), wsel),
                pl.BlockSpec((1, D_FF, D_MODEL), wsel),
                pl.BlockSpec((1, 1, D_MODEL), wsel),
            ],
            out_specs=pl.BlockSpec((bm * ROW_TILE, LANES), orow),
            scratch_shapes=[
                pltpu.VMEM((bm, D_MODEL), BF16),
                pltpu.VMEM((D_MODEL, 2 * D_FF), BF16),
                pltpu.VMEM((D_FF, D_MODEL), BF16),
            ],
        ),
        out_shape=jax.ShapeDtypeStruct(xs.shape, U32),
        compiler_params=pltpu.CompilerParams(
            dimension_semantics=("arbitrary",), vmem_limit_bytes=VMEM_LIMIT),
        name="experts",
    )(block_e, n_used, xs, wgu, bgu, wd, bd)


def _combine_kernel(dest_hbm, ys_hbm, x1_ref, gate_ref, p_ref, gng_ref, wg_ref, wp_ref, png_ref,
                    fng_ref, o_ref, dsm, ybuf, sem_d, sem):
    i = pl.program_id(0)
    tr = x1_ref.shape[0]
    w = TOP_K * tr
    slot = lax.rem(i, 2)

    def dest_copy(step, s):
        return pltpu.make_async_copy(dest_hbm.at[step], dsm.at[pl.ds(pl.multiple_of(s * w, w), w)],
                                     sem_d)

    def row_copy(s, t, k):
        d = dsm[s * w + k * tr + t]
        return pltpu.make_async_copy(ys_hbm.at[_row_tile(d), :], ybuf.at[s, k, _row_tile(t), :],
                                     sem.at[s])

    def gather_tile(step, s):
        cp = dest_copy(step, s)
        cp.start()
        cp.wait()

        def start(t, c):
            for k in range(TOP_K):
                row_copy(s, t, k).start(priority=k % 2)
            return c

        lax.fori_loop(0, tr, start, 0, unroll=2)

    @pl.when(i == 0)
    def _():
        gather_tile(0, 0)

    @pl.when(i + 1 < pl.num_programs(0))
    def _():
        gather_tile(i + 1, 1 - slot)

    def wait(t, c):
        for k in range(TOP_K):
            row_copy(slot, t, k).wait()
        return c

    lax.fori_loop(0, tr, wait, 0, unroll=8)

    lo_chunks, hi_chunks = [], []
    for c in range(ROW_TILE):
        acc_lo = x1_ref[:, c * LANES:(c + 1) * LANES]
        acc_hi = x1_ref[:, HALF + c * LANES:HALF + (c + 1) * LANES]
        for k in range(TOP_K):
            lo, hi = _load_packed_chunk(ybuf.at[slot, k], c, tr)
            g = gate_ref[:, k:k + 1]
            acc_lo = acc_lo + g * lo
            acc_hi = acc_hi + g * hi
        lo_chunks.append(acc_lo)
        hi_chunks.append(acc_hi)
    x2 = jnp.concatenate(lo_chunks + hi_chunks, axis=1)
    gate = jnp.dot(_rms(x2, gng_ref[...]).astype(BF16), wg_ref[...], preferred_element_type=F32)
    gate = 1.0 / (1.0 + jnp.exp(-gate))
    e = jnp.dot(p_ref[...].astype(BF16), wp_ref[...], preferred_element_type=F32)
    x3 = x2 + _rms(e, png_ref[...]) * gate
    o_ref[...] = _rms(x3, fng_ref[...])


def _combine(dest2d, ys, x1, gate_t, p2d, gng, wg, wp, png, fng):
    t = x1.shape[0]
    nt, w = dest2d.shape
    tr = w // TOP_K
    c2 = lambda i: (0, 0)
    return pl.pallas_call(
        _combine_kernel,
        grid=(nt,),
        in_specs=[
            pl.BlockSpec(memory_space=pl.ANY),
            pl.BlockSpec(memory_space=pl.ANY),
            pl.BlockSpec((tr, D_MODEL), lambda i: (i, 0)),
            pl.BlockSpec((tr, TOP_K), lambda i: (i, 0)),
            pl.BlockSpec((tr, PLE_DIM), lambda i: (i, 0)),
            pl.BlockSpec((1, D_MODEL), c2),
            pl.BlockSpec((D_MODEL, D_MODEL), c2),
            pl.BlockSpec((PLE_DIM, D_MODEL), c2),
            pl.BlockSpec((1, D_MODEL), c2),
            pl.BlockSpec((1, D_MODEL), c2),
        ],
        out_specs=pl.BlockSpec((tr, D_MODEL), lambda i: (i, 0)),
        out_shape=jax.ShapeDtypeStruct((t, D_MODEL), F32),
        scratch_shapes=[
            pltpu.SMEM((2 * w,), I32),
            pltpu.VMEM((2, TOP_K, tr * ROW_TILE, LANES), U32),
            pltpu.SemaphoreType.DMA,
            pltpu.SemaphoreType.DMA((2,)),
        ],
        compiler_params=pltpu.CompilerParams(
            dimension_semantics=("arbitrary",), vmem_limit_bytes=VMEM_LIMIT),
        name="combine",
    )(dest2d, ys, x1, gate_t, p2d, gng, wg, wp, png, fng)


def _pad_heads(v):
    return jnp.pad(v.astype(F32), (0, HEAD_PAD - SSD_HEADS)).reshape(1, HEAD_PAD)


def _layer(x, p, mix_norm_g, w_in, conv_w, conv_b, dt_bias, a_log, d_skip, ssd_norm_g, pool_w,
           pool_scale, w_out, ffn_norm_g, router_w, router_b, w_gate_up, b_gate_up, w_down,
           b_down, ple_gate_norm_g, w_ple_gate, w_ple_proj, ple_norm_g, final_g):
    b, l, _ = x.shape
    t = b * l
    x2d = x.reshape(t, D_MODEL)
    row = lambda v: v.reshape(1, -1).astype(F32)

    c_dt = D_SSD + D_CONV
    c_pool = c_dt + SSD_HEADS
    w_main = w_in.astype(BF16)
    w_tail = jnp.concatenate(
        [w_in[:, c_pool:], w_in[:, c_dt:c_pool],
         jnp.zeros((D_MODEL, HEAD_PAD - SSD_HEADS), w_in.dtype)], axis=1).astype(BF16)
    ii = jnp.arange(CHUNK)
    tri = (ii[None, :] <= ii[:, None]).astype(BF16)
    rr = jnp.arange(CHUNK)[:, None] + POOL_HIST
    jj = jnp.arange(CHUNK + POOL_HIST)[None, :]
    band = jnp.stack([((jj <= rr) & (jj > rr - w)) for w in POOL_WINDOWS]).astype(BF16)
    tr = min(TR_ROUTE, t)
    ri = jnp.arange(tr)
    utri = (ri[:, None] <= ri[None, :]).astype(BF16)

    head_of_col = jnp.arange(D_SSD) // SSD_HEAD_DIM
    expand = (jnp.arange(HEAD_PAD)[:, None] == head_of_col[None, :]).astype(BF16)

    z, xbc, pool_in, dt = _inproj(x2d, row(mix_norm_g), w_main, w_tail)
    y_ssd, y_pool = _ssd(xbc.reshape(b, l, D_CONV), z.reshape(b, l, D_SSD),
                         dt.reshape(b, l, HEAD_PAD), conv_w.astype(F32), row(conv_b),
                         _pad_heads(dt_bias), _pad_heads(a_log),
                         row(jnp.repeat(d_skip, SSD_HEAD_DIM)), row(ssd_norm_g), tri, expand,
                         pool_in.reshape(b, l, D_POOL), band, pool_w.astype(BF16), row(pool_scale))
    x1, h, idx, gate, rank, cnt = _outproj(
        y_ssd.reshape(t, D_SSD), y_pool.reshape(t, D_POOL), x2d, w_out[:D_SSD].astype(BF16),
        w_out[D_SSD:].astype(BF16), row(ffn_norm_g), router_w.T.astype(BF16),
        router_b.reshape(N_EXPERTS, 1).astype(F32), utri)
    counts = cnt[:, 0].astype(I32)
    bm = BM_EXPERT
    padded = ((counts + bm - 1) // bm) * bm
    pend = jnp.cumsum(padded)
    pstart = (pend - padded).astype(I32)
    nb = (t * TOP_K) // bm + N_EXPERTS
    n_used = (pend[-1] // bm).astype(I32)
    blk = jnp.arange(nb, dtype=I32)
    block_e = jnp.sum((pend[None, :] <= (blk * bm)[:, None]).astype(I32), axis=1)
    block_e = jnp.minimum(block_e, N_EXPERTS - 1)
    last_e = jnp.sum((pend <= (n_used - 1) * bm).astype(I32))
    block_e = jnp.where(blk < n_used, block_e, jnp.minimum(last_e, N_EXPERTS - 1)).astype(I32)

    dest = _dest(pstart, idx, rank)
    nt = t // tr
    dest2d = dest.reshape(nt, TOP_K * tr)
    xs = _dispatch(counts, padded.astype(I32), pstart, n_used.reshape(1), dest2d, h, nb * bm)
    ys = _experts(block_e, n_used.reshape(1), xs, w_gate_up.astype(F32),
                  b_gate_up.reshape(N_EXPERTS, 1, 2 * D_FF).astype(F32), w_down.astype(F32),
                  b_down.reshape(N_EXPERTS, 1, D_MODEL).astype(F32))
    gate_t = gate.transpose(0, 2, 1).reshape(t, TOP_K)
    out = _combine(dest2d, ys, x1, gate_t, p.reshape(t, PLE_DIM), row(ple_gate_norm_g),
                   w_ple_gate.astype(BF16), w_ple_proj.astype(BF16), row(ple_norm_g), row(final_g))
    return out.reshape(b, l, D_MODEL)


def kernel(x, p, mix_norm_g, w_in, conv_w, conv_b, dt_bias, a_log, d_skip, ssd_norm_g, pool_w, pool_scale, w_out, ffn_norm_g, router_w, router_b, w_gate_up, b_gate_up, w_down, b_down, ple_gate_norm_g, w_ple_gate, w_ple_proj, ple_norm_g, final_norm_g):
    assert x.shape[-1] == D_MODEL and mix_norm_g.shape[0] == 1
    layer0 = lambda v: v.reshape(v.shape[1:])
    per_layer = (p, mix_norm_g, w_in, conv_w, conv_b, dt_bias, a_log, d_skip, ssd_norm_g, pool_w,
                 pool_scale, w_out, ffn_norm_g, router_w, router_b, w_gate_up, b_gate_up, w_down,
                 b_down, ple_gate_norm_g, w_ple_gate, w_ple_proj, ple_norm_g)
    return _layer(x, *[layer0(v) for v in per_layer], final_norm_g)
```

```python
import jax
import jax.numpy as jnp
from jax import lax
from jax.experimental import pallas as pl
from jax.experimental.pallas import tpu as pltpu

F32 = jnp.float32
BF16 = jnp.bfloat16
I32 = jnp.int32

D_MODEL = 1024
SSD_HEADS = 32
SSD_HEAD_DIM = 64
D_SSD = SSD_HEADS * SSD_HEAD_DIM
SSD_GROUPS = 8
HEADS_PER_GROUP = SSD_HEADS // SSD_GROUPS
GROUP_W = HEADS_PER_GROUP * SSD_HEAD_DIM
D_STATE = 128
CONV_WIDTH = 4
CHUNK = 128
D_BC = SSD_GROUPS * D_STATE
D_CONV = D_SSD + 2 * D_BC
POOL_WINDOWS = (2, 4, 8, 16)
POOL_CH = 256
D_POOL = len(POOL_WINDOWS) * POOL_CH
POOL_HIST = 128
N_EXPERTS = 32
TOP_K = 4
D_FF = 1024
SWIGLU_LIMIT = 7.0
SWIGLU_ALPHA = 1.702
PLE_DIM = 256
EPS = 1e-6

LANES = 128
HEAD_PAD = LANES

TM_IN = 512
SSD_STEP_CHUNKS = 4
TM_OUT = 512
TR_ROUTE = 256
DEST_TILES = 16
BM_EXPERT = 512
VMEM_LIMIT = 56 * 1024 * 1024


def _rms(x, g):
    return x * lax.rsqrt(jnp.mean(x * x, axis=-1, keepdims=True) + EPS) * g


def _silu(x):
    h = 0.5 * x
    return h + h * jnp.tanh(h)


HALF = D_MODEL // 2
ROW_TILE = HALF // LANES
U32 = jnp.uint32


def _store_packed_rows(ref, v):
    m = v.shape[0]
    u = lax.bitcast_convert_type(v, U32)
    r = u + jnp.uint32(0x7FFF) + ((u >> 16) & jnp.uint32(1))
    w = (r[:, :HALF] >> 16) | (r[:, HALF:] & jnp.uint32(0xFFFF0000))
    for c in range(ROW_TILE):
        ref[pl.ds(c, m, stride=ROW_TILE), :] = w[:, c * LANES:(c + 1) * LANES]


def _load_packed_chunk(ref, c, m):
    w = ref[pl.ds(c, m, stride=ROW_TILE), :]
    lo = lax.bitcast_convert_type(w << 16, F32)
    hi = lax.bitcast_convert_type(w & jnp.uint32(0xFFFF0000), F32)
    return lo, hi


def _split3(v):
    hi = v.astype(BF16)
    r1 = v - hi.astype(F32)
    mid = r1.astype(BF16)
    lo = (r1 - mid.astype(F32)).astype(BF16)
    return hi, mid, lo


D_IN = D_SSD + D_CONV + SSD_HEADS + D_POOL
W_TAIL_COLS = D_POOL + HEAD_PAD


def _inproj_kernel(x_ref, g_ref, wm_ref, wt_ref, z_ref, xbc_ref, pool_ref, dt_ref):
    hb = _rms(x_ref[...], g_ref[...]).astype(BF16)

    def mm(w_ref, c0, n):
        return jnp.dot(hb, w_ref[:, c0:c0 + n], preferred_element_type=F32)

    cw = 512
    for c in range(0, D_SSD, cw):
        z_ref[:, c:c + cw] = mm(wm_ref, c, cw).astype(BF16)
    for c in range(0, D_CONV, cw):
        xbc_ref[:, c:c + cw] = mm(wm_ref, D_SSD + c, cw).astype(BF16)
    for c in range(0, D_POOL, cw):
        pool_ref[:, c:c + cw] = mm(wt_ref, c, cw).astype(BF16)
    dt_ref[...] = mm(wt_ref, D_POOL, HEAD_PAD)


def _inproj(x2d, g, w_main, w_tail):
    t = x2d.shape[0]
    tm = min(TM_IN, t)
    return pl.pallas_call(
        _inproj_kernel,
        grid=(t // tm,),
        in_specs=[
            pl.BlockSpec((tm, D_MODEL), lambda i: (i, 0)),
            pl.BlockSpec((1, D_MODEL), lambda i: (0, 0)),
            pl.BlockSpec((D_MODEL, D_IN), lambda i: (0, 0), pipeline_mode=pl.Buffered(1)),
            pl.BlockSpec((D_MODEL, W_TAIL_COLS), lambda i: (0, 0), pipeline_mode=pl.Buffered(1)),
        ],
        out_specs=[
            pl.BlockSpec((tm, D_SSD), lambda i: (i, 0)),
            pl.BlockSpec((tm, D_CONV), lambda i: (i, 0)),
            pl.BlockSpec((tm, D_POOL), lambda i: (i, 0)),
            pl.BlockSpec((tm, HEAD_PAD), lambda i: (i, 0)),
        ],
        out_shape=[
            jax.ShapeDtypeStruct((t, D_SSD), BF16),
            jax.ShapeDtypeStruct((t, D_CONV), BF16),
            jax.ShapeDtypeStruct((t, D_POOL), BF16),
            jax.ShapeDtypeStruct((t, HEAD_PAD), F32),
        ],
        compiler_params=pltpu.CompilerParams(
            dimension_semantics=("arbitrary",), vmem_limit_bytes=VMEM_LIMIT),
        name="inproj",
    )(x2d, g, w_main, w_tail)


def _ssd_kernel(xbc_ref, z_ref, dt_ref, convw_ref, convb_ref, dtb_ref, alog_ref, dskip_ref,
                ng_ref, tri_ref, expand_ref, u_ref, band_ref, pw_ref, pscale_ref, y_ref, yp_ref,
                state_ref, hist_ref, xs_s, b_s, c_s, phist_ref):
    q = CHUNK
    n_sub = xbc_ref.shape[1] // q

    @pl.when(pl.program_id(1) == 0)
    def _():
        state_ref[...] = jnp.zeros_like(state_ref)
        hist_ref[...] = jnp.zeros_like(hist_ref)
        phist_ref[0:POOL_HIST, :] = jnp.zeros((POOL_HIST, D_POOL), BF16)

    def one_chunk(sub, carry):
        rows = pl.ds(pl.multiple_of(sub * q, q), q)
        li = pl.program_id(1) * n_sub + sub

        phist_ref[POOL_HIST:POOL_HIST + q, :] = u_ref[0, rows, :]
        pos = li * q + lax.broadcasted_iota(I32, (q, POOL_CH), 0)
        for gi, w in enumerate(POOL_WINDOWS):
            ps = slice(gi * POOL_CH, (gi + 1) * POOL_CH)
            win_sum = jnp.dot(band_ref[gi], phist_ref[:, ps], preferred_element_type=F32)
            cnt = jnp.minimum(pos + 1, w).astype(F32)
            pre = (win_sum / cnt - u_ref[0, rows, ps].astype(F32)).astype(BF16)
            yp = jnp.dot(pre, pw_ref[gi], preferred_element_type=F32) * pscale_ref[:, ps]
            yp_ref[0, rows, ps] = yp.astype(BF16)
        phist_ref[0:POOL_HIST, :] = phist_ref[q:q + POOL_HIST, :]

        cw = GROUP_W
        hr = hist_ref.shape[0]
        for ci in range(D_CONV // cw):
            cs = slice(ci * cw, (ci + 1) * cw)
            cur = xbc_ref[0, rows, cs].astype(F32)
            xe = jnp.concatenate([hist_ref[:, cs], cur], axis=0)
            hist_ref[:, cs] = cur[q - hr:q, :]
            s1 = pltpu.roll(xe, 1, 0)
            u = convw_ref[3:4, cs] * xe + convw_ref[2:3, cs] * s1
            w = convw_ref[1:2, cs] * xe + convw_ref[0:1, cs] * s1
            acc = convb_ref[:, cs] + u + pltpu.roll(w, 2, 0)
            v = _silu(acc[hr:, :])
            if ci < D_SSD // cw:
                xs_s[ci] = v
            elif ci < (D_SSD + D_BC) // cw:
                j = ci - D_SSD // cw
                b_s[2 * j] = v[:, :D_STATE]
                b_s[2 * j + 1] = v[:, D_STATE:]
            else:
                j = ci - (D_SSD + D_BC) // cw
                c_s[2 * j] = v[:, :D_STATE].astype(BF16)
                c_s[2 * j + 1] = v[:, D_STATE:].astype(BF16)

        xdt = dt_ref[0, rows, :] + dtb_ref[...]
        dt = jnp.maximum(xdt, 0.0) + jnp.log1p(jnp.exp(-jnp.abs(xdt)))
        a = dt * (-jnp.exp(alog_ref[...]))
        tri = tri_ref[...]
        a_hi, a_mid, a_lo = _split3(a)
        acs = (jnp.dot(tri, a_hi, preferred_element_type=F32)
               + jnp.dot(tri, a_mid, preferred_element_type=F32)
               + jnp.dot(tri, a_lo, preferred_element_type=F32))
        acs_t = acs.T
        dt_t = dt.T
        eacs = jnp.exp(acs)
        w_t = jnp.exp(acs_t[:, q - 1:q] - acs_t) * dt_t
        e_hi = eacs.astype(BF16)
        e_lo = (eacs - e_hi.astype(F32)).astype(BF16)

        row = lax.broadcasted_iota(I32, (q, q), 0)
        col = lax.broadcasted_iota(I32, (q, q), 1)
        causal = col <= row
        head_shift = SSD_HEAD_DIM.bit_length() - 1
        lane_head = jnp.right_shift(lax.broadcasted_iota(I32, (q, GROUP_W), 1), head_shift)
        head_mask = [jnp.where(lane_head == r, 1.0, 0.0).astype(BF16)
                     for r in range(HEADS_PER_GROUP)]

        for g in range(SSD_GROUPS):
            gs = slice(g * GROUP_W, (g + 1) * GROUP_W)
            xs_g = xs_s[g]
            b_g = b_s[g]
            c_g = c_s[g]
            cb = lax.dot_general(c_g, b_g.astype(BF16), (((1,), (1,)), ((), ())),
                                 preferred_element_type=F32)
            bt_g = b_g.T
            s_old = state_ref[g]
            y_off = jnp.dot(c_g, s_old.astype(BF16), preferred_element_type=F32)
            esc = (jnp.dot(e_hi, expand_ref[:, gs], preferred_element_type=F32)
                   + jnp.dot(e_lo, expand_ref[:, gs], preferred_element_type=F32))
            xs_b = xs_g.astype(BF16)
            m_cat, bts_cat = [], []
            for r in range(HEADS_PER_GROUP):
                h = g * HEADS_PER_GROUP + r
                seg = acs[:, h:h + 1] - acs_t[h:h + 1, :]
                dec = jnp.where(causal, jnp.exp(seg), 0.0)
                m_cat.append((cb * dec * dt_t[h:h + 1, :]).astype(BF16))
                bts_cat.append((bt_g * w_t[h:h + 1, :]).astype(BF16))
            xs_bd = jnp.concatenate([xs_b * head_mask[r] for r in range(HEADS_PER_GROUP)], axis=0)
            y_diag = jnp.dot(jnp.concatenate(m_cat, axis=1), xs_bd, preferred_element_type=F32)
            ds = jnp.dot(jnp.concatenate(bts_cat, axis=1), xs_bd, preferred_element_type=F32)
            state_ref[g] = s_old * esc[q - 1:q, :] + ds
            y = y_diag + y_off * esc + dskip_ref[:, gs] * xs_g
            y = y * _silu(z_ref[0, rows, gs].astype(F32))
            y_ref[0, rows, gs] = _rms(y, ng_ref[:, gs]).astype(BF16)
        return carry

    lax.fori_loop(0, n_sub, one_chunk, 0)


def _ssd(xbc, z, dt, conv_w, conv_b, dtb, alog, dskip, ng, tri, expand, u, band, pw, pscale):
    b, l, _ = xbc.shape
    rows = SSD_STEP_CHUNKS * CHUNK if l % (SSD_STEP_CHUNKS * CHUNK) == 0 else CHUNK
    cmap = lambda i, c: (0, 0)
    cmap3 = lambda i, c: (0, 0, 0)
    blk = lambda i, c: (i, c, 0)
    return pl.pallas_call(
        _ssd_kernel,
        grid=(b, l // rows),
        in_specs=[
            pl.BlockSpec((1, rows, D_CONV), blk),
            pl.BlockSpec((1, rows, D_SSD), blk),
            pl.BlockSpec((1, rows, HEAD_PAD), blk),
            pl.BlockSpec((CONV_WIDTH, D_CONV), cmap),
            pl.BlockSpec((1, D_CONV), cmap),
            pl.BlockSpec((1, HEAD_PAD), cmap),
            pl.BlockSpec((1, HEAD_PAD), cmap),
            pl.BlockSpec((1, D_SSD), cmap),
            pl.BlockSpec((1, D_SSD), cmap),
            pl.BlockSpec((CHUNK, CHUNK), cmap),
            pl.BlockSpec((HEAD_PAD, D_SSD), cmap),
            pl.BlockSpec((1, rows, D_POOL), blk),
            pl.BlockSpec((len(POOL_WINDOWS), CHUNK, CHUNK + POOL_HIST), cmap3),
            pl.BlockSpec((len(POOL_WINDOWS), POOL_CH, POOL_CH), cmap3),
            pl.BlockSpec((1, D_POOL), cmap),
        ],
        out_specs=[pl.BlockSpec((1, rows, D_SSD), blk), pl.BlockSpec((1, rows, D_POOL), blk)],
        out_shape=[jax.ShapeDtypeStruct((b, l, D_SSD), BF16),
                   jax.ShapeDtypeStruct((b, l, D_POOL), BF16)],
        scratch_shapes=[
            pltpu.VMEM((SSD_GROUPS, D_STATE, GROUP_W), F32),
            pltpu.VMEM((8, D_CONV), F32),
            pltpu.VMEM((SSD_GROUPS, CHUNK, GROUP_W), F32),
            pltpu.VMEM((SSD_GROUPS, CHUNK, D_STATE), F32),
            pltpu.VMEM((SSD_GROUPS, CHUNK, D_STATE), BF16),
            pltpu.VMEM((POOL_HIST + CHUNK, D_POOL), BF16),
        ],
        compiler_params=pltpu.CompilerParams(
            dimension_semantics=("arbitrary", "arbitrary"), vmem_limit_bytes=VMEM_LIMIT),
        name="ssd",
    )(xbc, z, dt, conv_w, conv_b, dtb, alog, dskip, ng, tri, expand, u, band, pw, pscale)


def _outproj_kernel(ys_ref, yp_ref, x_ref, ws_ref, wp_ref, g_ref, rwt_ref, rb_ref, utri_ref,
                    x1_ref, h_ref, idx_ref, gate_ref, rank_ref, cnt_ref, carry_ref):
    @pl.when(pl.program_id(0) == 0)
    def _():
        carry_ref[...] = jnp.zeros_like(carry_ref)

    acc = jnp.dot(ys_ref[...], ws_ref[...], preferred_element_type=F32)
    acc = acc + jnp.dot(yp_ref[...], wp_ref[...], preferred_element_type=F32)
    x1 = x_ref[...] + acc
    x1_ref[...] = x1
    h = _rms(x1, g_ref[...])
    _store_packed_rows(h_ref, h)
    lt = lax.dot_general(rwt_ref[...], h.astype(BF16), (((1,), (1,)), ((), ())),
                         preferred_element_type=F32) + rb_ref[...]
    tr = utri_ref.shape[0]
    for s in range(idx_ref.shape[0]):
        _route_tile(lt[:, s * tr:(s + 1) * tr], utri_ref, idx_ref.at[s], gate_ref.at[s],
                    rank_ref.at[s], carry_ref)
    cnt_ref[...] = carry_ref[...]


def _outproj(y_ssd, y_pool, x2d, w_s, w_p, g, rwt, rb, utri):
    t = x2d.shape[0]
    tm = min(TM_OUT, t)
    tr = utri.shape[0]
    nt = t // tr
    c2 = lambda i: (0, 0)
    o3 = lambda i: (i, 0, 0)
    return pl.pallas_call(
        _outproj_kernel,
        grid=(t // tm,),
        in_specs=[
            pl.BlockSpec((tm, D_SSD), lambda i: (i, 0)),
            pl.BlockSpec((tm, D_POOL), lambda i: (i, 0)),
            pl.BlockSpec((tm, D_MODEL), lambda i: (i, 0)),
            pl.BlockSpec((D_SSD, D_MODEL), c2),
            pl.BlockSpec((D_POOL, D_MODEL), c2),
            pl.BlockSpec((1, D_MODEL), c2),
            pl.BlockSpec((N_EXPERTS, D_MODEL), c2),
            pl.BlockSpec((N_EXPERTS, 1), c2),
            pl.BlockSpec((tr, tr), c2),
        ],
        out_specs=[
            pl.BlockSpec((tm, D_MODEL), lambda i: (i, 0)),
            pl.BlockSpec((tm * ROW_TILE, LANES), lambda i: (i, 0)),
            pl.BlockSpec((tm // tr, TOP_K, tr), o3),
            pl.BlockSpec((tm // tr, TOP_K, tr), o3),
            pl.BlockSpec((tm // tr, TOP_K, tr), o3),
            pl.BlockSpec((N_EXPERTS, LANES), c2),
        ],
        out_shape=[
            jax.ShapeDtypeStruct((t, D_MODEL), F32),
            jax.ShapeDtypeStruct((t * ROW_TILE, LANES), U32),
            jax.ShapeDtypeStruct((nt, TOP_K, tr), I32),
            jax.ShapeDtypeStruct((nt, TOP_K, tr), F32),
            jax.ShapeDtypeStruct((nt, TOP_K, tr), I32),
            jax.ShapeDtypeStruct((N_EXPERTS, LANES), F32),
        ],
        scratch_shapes=[pltpu.VMEM((N_EXPERTS, LANES), F32)],
        compiler_params=pltpu.CompilerParams(
            dimension_semantics=("arbitrary",), vmem_limit_bytes=VMEM_LIMIT),
        name="outproj",
    )(y_ssd, y_pool, x2d, w_s, w_p, g, rwt, rb, utri)


def _route_tile(l, utri_ref, idx_ref, gate_ref, rank_ref, carry_ref):
    tr = l.shape[1]
    eidx = lax.broadcasted_iota(I32, (N_EXPERTS, tr), 0).astype(F32)
    sels, vals, idxs = [], [], []
    for _ in range(TOP_K):
        m = jnp.max(l, axis=0, keepdims=True)
        ik = jnp.min(jnp.where(l == m, eidx, float(N_EXPERTS)), axis=0, keepdims=True)
        sel = eidx == ik
        l = jnp.where(sel, -jnp.inf, l)
        sels.append(sel)
        vals.append(m)
        idxs.append(ik)
    es = [jnp.exp(v - vals[0]) for v in vals]
    den = es[0] + es[1] + es[2] + es[3]
    multi = jnp.zeros((N_EXPERTS, tr), F32)
    for sel in sels:
        multi = multi + jnp.where(sel, 1.0, 0.0)
    incl = jnp.dot(multi.astype(BF16), utri_ref[...], preferred_element_type=F32)
    excl = incl - multi + carry_ref[:, 0:1]
    for k in range(TOP_K):
        idx_ref[k:k + 1, :] = idxs[k].astype(I32)
        gate_ref[k:k + 1, :] = es[k] / den
        rk = jnp.sum(jnp.where(sels[k], excl, 0.0), axis=0, keepdims=True)
        rank_ref[k:k + 1, :] = rk.astype(I32)
    carry_ref[...] = carry_ref[...] + jnp.sum(multi, axis=1, keepdims=True)


def _dest_kernel(pstart_ref, idx_ref, rank_ref, dest_ref):
    idx = idx_ref[...]
    base = jnp.zeros_like(idx)
    for e in range(N_EXPERTS):
        base = jnp.where(idx == e, pstart_ref[e], base)
    dest_ref[...] = base + rank_ref[...]


def _dest(pstart, idx, rank):
    nt, _, tr = idx.shape
    tb = min(DEST_TILES, nt)
    o3 = lambda i, ps: (i, 0, 0)
    return pl.pallas_call(
        _dest_kernel,
        grid_spec=pltpu.PrefetchScalarGridSpec(
            num_scalar_prefetch=1,
            grid=(nt // tb,),
            in_specs=[pl.BlockSpec((tb, TOP_K, tr), o3), pl.BlockSpec((tb, TOP_K, tr), o3)],
            out_specs=pl.BlockSpec((tb, TOP_K, tr), o3),
        ),
        out_shape=jax.ShapeDtypeStruct((nt, TOP_K, tr), I32),
        compiler_params=pltpu.CompilerParams(dimension_semantics=("arbitrary",)),
        name="dest",
    )(pstart, idx, rank)


def _row_tile(r):
    return pl.ds(pl.multiple_of(r * ROW_TILE, ROW_TILE), ROW_TILE)


def _dispatch_kernel(cnt_ref, pad_ref, pstart_ref, nu_ref, dest_hbm, h_ref, xs_hbm, dsm, zrow, sem_d,
                     sem, sem_z):
    i = pl.program_id(0)
    w = dsm.shape[0] // 2
    tr = w // TOP_K
    zrows = zrow.shape[0]
    nb = xs_hbm.shape[0] // zrows
    slot = lax.rem(i, 2)

    def dest_copy(step, s):
        return pltpu.make_async_copy(dest_hbm.at[step], dsm.at[pl.ds(pl.multiple_of(s * w, w), w)],
                                     sem_d)

    def zero_block_copy(blk):
        rows = pl.ds(pl.multiple_of(blk * zrows, zrows), zrows)
        return pltpu.make_async_copy(zrow, xs_hbm.at[rows, :], sem_z)

    @pl.when(i == 0)
    def _():
        dest_copy(0, 0).start()
        zrow[...] = jnp.zeros_like(zrow)

        def zstart(blk, c):
            zero_block_copy(blk).start()
            return c

        def zwait(blk, c):
            zero_block_copy(blk).wait()
            return c

        lax.fori_loop(nu_ref[0], nb, zstart, 0)
        lax.fori_loop(nu_ref[0], nb, zwait, 0)
        run_sizes = [1 << s for s in reversed(range((zrows // ROW_TILE).bit_length() - 1))]

        def zero_run_copy(first_row, size):
            rows = pl.ds(first_row * ROW_TILE, size * ROW_TILE)
            return pltpu.make_async_copy(zrow.at[pl.ds(0, size * ROW_TILE), :], xs_hbm.at[rows, :],
                                         sem_z)

        for e in range(N_EXPERTS):
            n_pad = pad_ref[e] - cnt_ref[e]
            row = pstart_ref[e] + cnt_ref[e]
            for size in run_sizes:
                @pl.when(jnp.bitwise_and(n_pad, size) != 0)
                def _():
                    zero_run_copy(row, size).start()

                row = row + jnp.bitwise_and(n_pad, size)
        for e in range(N_EXPERTS):
            n_pad = pad_ref[e] - cnt_ref[e]
            for size in run_sizes:
                @pl.when(jnp.bitwise_and(n_pad, size) != 0)
                def _():
                    zero_run_copy(0, size).wait()

    dest_copy(i, slot).wait()

    @pl.when(i + 1 < pl.num_programs(0))
    def _():
        dest_copy(i + 1, 1 - slot).start()

    def row_copy(t, k):
        d = dsm[slot * w + k * tr + t]
        return pltpu.make_async_copy(h_ref.at[_row_tile(t), :], xs_hbm.at[_row_tile(d), :], sem)

    def start(t, c):
        for k in range(TOP_K):
            row_copy(t, k).start(priority=k % 2)
        return c

    def wait(t, c):
        for k in range(TOP_K):
            row_copy(t, k).wait()
        return c

    lax.fori_loop(0, tr, start, 0, unroll=2)
    lax.fori_loop(0, tr, wait, 0, unroll=8)


def _dispatch(counts, padded, pstart, n_used, dest2d, h, n_slots):
    nt, w = dest2d.shape
    tr = w // TOP_K
    return pl.pallas_call(
        _dispatch_kernel,
        grid_spec=pltpu.PrefetchScalarGridSpec(
            num_scalar_prefetch=4,
            grid=(nt,),
            in_specs=[
                pl.BlockSpec(memory_space=pl.ANY),
                pl.BlockSpec((tr * ROW_TILE, LANES), lambda i, *_: (i, 0)),
            ],
            out_specs=pl.BlockSpec(memory_space=pl.ANY),
            scratch_shapes=[
                pltpu.SMEM((2 * w,), I32),
                pltpu.VMEM((BM_EXPERT * ROW_TILE, LANES), U32),
                pltpu.SemaphoreType.DMA,
                pltpu.SemaphoreType.DMA,
                pltpu.SemaphoreType.DMA,
            ],
        ),
        out_shape=jax.ShapeDtypeStruct((n_slots * ROW_TILE, LANES), U32),
        compiler_params=pltpu.CompilerParams(dimension_semantics=("arbitrary",)),
        name="dispatch",
    )(counts, padded, pstart, n_used, dest2d, h)


def _expert_kernel(be_ref, nu_ref, xs_ref, wgu_ref, bgu_ref, wd_ref, bd_ref, ys_ref, xb_s, wgu_s,
                   wd_s):
    i = pl.program_id(0)
    bm = xb_s.shape[0]

    @pl.when((i == 0) | (be_ref[i] != be_ref[jnp.maximum(i - 1, 0)]))
    def _():
        rc = 128
        for r in range(0, D_MODEL, rc):
            wgu_s[r:r + rc, :] = wgu_ref[0, r:r + rc, :].astype(BF16)
        for r in range(0, D_FF, rc):
            wd_s[r:r + rc, :] = wd_ref[0, r:r + rc, :].astype(BF16)

    @pl.when(i < nu_ref[0])
    def _():
        for c in range(ROW_TILE):
            lo, hi = _load_packed_chunk(xs_ref, c, bm)
            xb_s[:, c * LANES:(c + 1) * LANES] = lo.astype(BF16)
            xb_s[:, HALF + c * LANES:HALF + (c + 1) * LANES] = hi.astype(BF16)
        gu = jnp.dot(xb_s[...], wgu_s[...], preferred_element_type=F32) + bgu_ref[0]
        gt = jnp.minimum(gu[:, :D_FF], SWIGLU_LIMIT)
        up = jnp.clip(gu[:, D_FF:], -SWIGLU_LIMIT, SWIGLU_LIMIT)
        hg = 0.5 * gt
        act = (up + 1.0) * (hg + hg * jnp.tanh(SWIGLU_ALPHA * hg))
        y = jnp.dot(act.astype(BF16), wd_s[...], preferred_element_type=F32) + bd_ref[0]
        _store_packed_rows(ys_ref, y)

    @pl.when(i >= nu_ref[0])
    def _():
        ys_ref[...] = jnp.zeros_like(ys_ref)


def _experts(block_e, n_used, xs, wgu, bgu, wd, bd):
    bm = BM_EXPERT
    nb = xs.shape[0] // (bm * ROW_TILE)
    row = lambda i, be, nu: (jnp.minimum(i, nu[0] - 1), 0)
    orow = lambda i, be, nu: (i, 0)
    wsel = lambda i, be, nu: (be[i], 0, 0)
    return pl.pallas_call(
        _expert_kernel,
        grid_spec=pltpu.PrefetchScalarGridSpec(
            num_scalar_prefetch=2,
            grid=(nb,),
            in_specs=[
                pl.BlockSpec((bm * ROW_TILE, LANES), row),
                pl.BlockSpec((1, D_MODEL, 2 * D_FF), wsel),
                pl.BlockSpec((1, 1, 2 * D_FF), wsel),
                pl.BlockSpec((1, D_FF, D_MODEL), wsel),
                pl.BlockSpec((1, 1, D_MODEL), wsel),
            ],
            out_specs=pl.BlockSpec((bm * ROW_TILE, LANES), orow),
            scratch_shapes=[
                pltpu.VMEM((bm, D_MODEL), BF16),
                pltpu.VMEM((D_MODEL, 2 * D_FF), BF16),
                pltpu.VMEM((D_FF, D_MODEL), BF16),
            ],
        ),
        out_shape=jax.ShapeDtypeStruct(xs.shape, U32),
        compiler_params=pltpu.CompilerParams(
            dimension_semantics=("arbitrary",), vmem_limit_bytes=VMEM_LIMIT),
        name="experts",
    )(block_e, n_used, xs, wgu, bgu, wd, bd)


def _combine_kernel(dest_hbm, ys_hbm, x1_ref, gate_ref, p_ref, gng_ref, wg_ref, wp_ref, png_ref,
                    fng_ref, o_ref, dsm, ybuf, sem_d, sem):
    i = pl.program_id(0)
    tr = x1_ref.shape[0]
    w = TOP_K * tr
    slot = lax.rem(i, 2)

    def dest_copy(step, s):
        return pltpu.make_async_copy(dest_hbm.at[step], dsm.at[pl.ds(pl.multiple_of(s * w, w), w)],
                                     sem_d)

    def row_copy(s, t, k):
        d = dsm[s * w + k * tr + t]
        return pltpu.make_async_copy(ys_hbm.at[_row_tile(d), :], ybuf.at[s, k, _row_tile(t), :],
                                     sem.at[s])

    def gather_tile(step, s):
        cp = dest_copy(step, s)
        cp.start()
        cp.wait()

        def start(t, c):
            for k in range(TOP_K):
                row_copy(s, t, k).start(priority=k % 2)
            return c

        lax.fori_loop(0, tr, start, 0, unroll=2)

    @pl.when(i == 0)
    def _():
        gather_tile(0, 0)

    @pl.when(i + 1 < pl.num_programs(0))
    def _():
        gather_tile(i + 1, 1 - slot)

    def wait(t, c):
        for k in range(TOP_K):
            row_copy(slot, t, k).wait()
        return c

    lax.fori_loop(0, tr, wait, 0, unroll=8)

    lo_chunks, hi_chunks = [], []
    for c in range(ROW_TILE):
        acc_lo = x1_ref[:, c * LANES:(c + 1) * LANES]
        acc_hi = x1_ref[:, HALF + c * LANES:HALF + (c + 1) * LANES]
        for k in range(TOP_K):
            lo, hi = _load_packed_chunk(ybuf.at[slot, k], c, tr)
            g = gate_ref[:, k:k + 1]
            acc_lo = acc_lo + g * lo
            acc_hi = acc_hi + g * hi
        lo_chunks.append(acc_lo)
        hi_chunks.append(acc_hi)
    x2 = jnp.concatenate(lo_chunks + hi_chunks, axis=1)
    gate = jnp.dot(_rms(x2, gng_ref[...]).astype(BF16), wg_ref[...], preferred_element_type=F32)
    gate = 1.0 / (1.0 + jnp.exp(-gate))
    e = jnp.dot(p_ref[...].astype(BF16), wp_ref[...], preferred_element_type=F32)
    x3 = x2 + _rms(e, png_ref[...]) * gate
    o_ref[...] = _rms(x3, fng_ref[...])


def _combine(dest2d, ys, x1, gate_t, p2d, gng, wg, wp, png, fng):
    t = x1.shape[0]
    nt, w = dest2d.shape
    tr = w // TOP_K
    c2 = lambda i: (0, 0)
    return pl.pallas_call(
        _combine_kernel,
        grid=(nt,),
        in_specs=[
            pl.BlockSpec(memory_space=pl.ANY),
            pl.BlockSpec(memory_space=pl.ANY),
            pl.BlockSpec((tr, D_MODEL), lambda i: (i, 0)),
            pl.BlockSpec((tr, TOP_K), lambda i: (i, 0)),
            pl.BlockSpec((tr, PLE_DIM), lambda i: (i, 0)),
            pl.BlockSpec((1, D_MODEL), c2),
            pl.BlockSpec((D_MODEL, D_MODEL), c2),
            pl.BlockSpec((PLE_DIM, D_MODEL), c2),
            pl.BlockSpec((1, D_MODEL), c2),
            pl.BlockSpec((1, D_MODEL), c2),
        ],
        out_specs=pl.BlockSpec((tr, D_MODEL), lambda i: (i, 0)),
        out_shape=jax.ShapeDtypeStruct((t, D_MODEL), F32),
        scratch_shapes=[
            pltpu.SMEM((2 * w,), I32),
            pltpu.VMEM((2, TOP_K, tr * ROW_TILE, LANES), U32),
            pltpu.SemaphoreType.DMA,
            pltpu.SemaphoreType.DMA((2,)),
        ],
        compiler_params=pltpu.CompilerParams(
            dimension_semantics=("arbitrary",), vmem_limit_bytes=VMEM_LIMIT),
        name="combine",
    )(dest2d, ys, x1, gate_t, p2d, gng, wg, wp, png, fng)


def _pad_heads(v):
    return jnp.pad(v.astype(F32), (0, HEAD_PAD - SSD_HEADS)).reshape(1, HEAD_PAD)


def _layer(x, p, mix_norm_g, w_in, conv_w, conv_b, dt_bias, a_log, d_skip, ssd_norm_g, pool_w,
           pool_scale, w_out, ffn_norm_g, router_w, router_b, w_gate_up, b_gate_up, w_down,
           b_down, ple_gate_norm_g, w_ple_gate, w_ple_proj, ple_norm_g, final_g):
    b, l, _ = x.shape
    t = b * l
    x2d = x.reshape(t, D_MODEL)
    row = lambda v: v.reshape(1, -1).astype(F32)

    c_dt = D_SSD + D_CONV
    c_pool = c_dt + SSD_HEADS
    w_main = w_in.astype(BF16)
    w_tail = jnp.concatenate(
        [w_in[:, c_pool:], w_in[:, c_dt:c_pool],
         jnp.zeros((D_MODEL, HEAD_PAD - SSD_HEADS), w_in.dtype)], axis=1).astype(BF16)
    ii = jnp.arange(CHUNK)
    tri = (ii[None, :] <= ii[:, None]).astype(BF16)
    rr = jnp.arange(CHUNK)[:, None] + POOL_HIST
    jj = jnp.arange(CHUNK + POOL_HIST)[None, :]
    band = jnp.stack([((jj <= rr) & (jj > rr - w)) for w in POOL_WINDOWS]).astype(BF16)
    tr = min(TR_ROUTE, t)
    ri = jnp.arange(tr)
    utri = (ri[:, None] <= ri[None, :]).astype(BF16)

    head_of_col = jnp.arange(D_SSD) // SSD_HEAD_DIM
    expand = (jnp.arange(HEAD_PAD)[:, None] == head_of_col[None, :]).astype(BF16)

    z, xbc, pool_in, dt = _inproj(x2d, row(mix_norm_g), w_main, w_tail)
    y_ssd, y_pool = _ssd(xbc.reshape(b, l, D_CONV), z.reshape(b, l, D_SSD),
                         dt.reshape(b, l, HEAD_PAD), conv_w.astype(F32), row(conv_b),
                         _pad_heads(dt_bias), _pad_heads(a_log),
                         row(jnp.repeat(d_skip, SSD_HEAD_DIM)), row(ssd_norm_g), tri, expand,
                         pool_in.reshape(b, l, D_POOL), band, pool_w.astype(BF16), row(pool_scale))
    x1, h, idx, gate, rank, cnt = _outproj(
        y_ssd.reshape(t, D_SSD), y_pool.reshape(t, D_POOL), x2d, w_out[:D_SSD].astype(BF16),
        w_out[D_SSD:].astype(BF16), row(ffn_norm_g), router_w.T.astype(BF16),
        router_b.reshape(N_EXPERTS, 1).astype(F32), utri)
    counts = cnt[:, 0].astype(I32)
    bm = BM_EXPERT
    padded = ((counts + bm - 1) // bm) * bm
    pend = jnp.cumsum(padded)
    pstart = (pend - padded).astype(I32)
    nb = (t * TOP_K) // bm + N_EXPERTS
    n_used = (pend[-1] // bm).astype(I32)
    blk = jnp.arange(nb, dtype=I32)
    block_e = jnp.sum((pend[None, :] <= (blk * bm)[:, None]).astype(I32), axis=1)
    block_e = jnp.minimum(block_e, N_EXPERTS - 1)
    last_e = jnp.sum((pend <= (n_used - 1) * bm).astype(I32))
    block_e = jnp.where(blk < n_used, block_e, jnp.minimum(last_e, N_EXPERTS - 1)).astype(I32)

    dest = _dest(pstart, idx, rank)
    nt = t // tr
    dest2d = dest.reshape(nt, TOP_K * tr)
    xs = _dispatch(counts, padded.astype(I32), pstart, n_used.reshape(1), dest2d, h, nb * bm)
    ys = _experts(block_e, n_used.reshape(1), xs, w_gate_up.astype(F32),
                  b_gate_up.reshape(N_EXPERTS, 1, 2 * D_FF).astype(F32), w_down.astype(F32),
                  b_down.reshape(N_EXPERTS, 1, D_MODEL).astype(F32))
    gate_t = gate.transpose(0, 2, 1).reshape(t, TOP_K)
    out = _combine(dest2d, ys, x1, gate_t, p.reshape(t, PLE_DIM), row(ple_gate_norm_g),
                   w_ple_gate.astype(BF16), w_ple_proj.astype(BF16), row(ple_norm_g), row(final_g))
    return out.reshape(b, l, D_MODEL)


def kernel(x, p, mix_norm_g, w_in, conv_w, conv_b, dt_bias, a_log, d_skip, ssd_norm_g, pool_w, pool_scale, w_out, ffn_norm_g, router_w, router_b, w_gate_up, b_gate_up, w_down, b_down, ple_gate_norm_g, w_ple_gate, w_ple_proj, ple_norm_g, final_norm_g):
    assert x.shape[-1] == D_MODEL and mix_norm_g.shape[0] == 1
    layer0 = lambda v: v.reshape(v.shape[1:])
    per_layer = (p, mix_norm_g, w_in, conv_w, conv_b, dt_bias, a_log, d_skip, ssd_norm_g, pool_w,
                 pool_scale, w_out, ffn_norm_g, router_w, router_b, w_gate_up, b_gate_up, w_down,
                 b_down, ple_gate_norm_g, w_ple_gate, w_ple_proj, ple_norm_g)
    return _layer(x, *[layer0(v) for v in per_layer], final_norm_g)
```
